```python
import jax, jax.numpy as jnp
from jax import lax
import numpy as np

D_MODEL = 4096
BATCH = 4
SEQ = 2048
DEPTH = 4
DEC_BATCH = 8
DEC_SEQ = 4
PAST_LEN = 8192
PAGE_SIZE = 128

HEAD_DIM = 128
N_GROUPS_A = 3
WINDOWS = (128, 512, 2048)
DILATIONS = (1, 4, 16)
HEADS_PER_GROUP = D_MODEL // HEAD_DIM // 2
A_WIDTH = HEADS_PER_GROUP * HEAD_DIM
QKV_A_WIDTH = N_GROUPS_A * 3 * A_WIDTH
N_HEADS_B = D_MODEL // HEAD_DIM
B_WIDTH = N_HEADS_B * HEAD_DIM
ROT_DIM = HEAD_DIM // 4
ROPE_THETA = 500000.0
D_FF = 256 * ((8 * D_MODEL // 3 + 255) // 256)
CONV_W = 3
SB_BLOCK = 128
SB_BIAS_INIT = -5.0
N_A_LAYERS = DEPTH // 2
N_B_LAYERS = DEPTH - N_A_LAYERS
RMS_EPS = 1e-6
NEG_BIG = -1e30

kernel_name = 'dilated_stickbreak_yoco_convffn'


def _rms(x, g):
    x32 = x.astype(jnp.float32)
    y = x32 * lax.rsqrt(jnp.mean(x32 * x32, axis=-1, keepdims=True) + RMS_EPS)
    return (y * g.astype(jnp.float32)).astype(x.dtype)


def _ada(c, w, b, n_chunks):
    mod = (jax.nn.silu(c) @ w + b)[:, None, :]
    return jnp.split(mod, n_chunks, axis=-1)


def _rope(x, pos):
    half = ROT_DIM // 2
    inv = ROPE_THETA ** (-jnp.arange(half, dtype=jnp.float32) / half)
    ang = pos.astype(jnp.float32)[:, None] * inv[None, :]
    cos = jnp.cos(ang)[:, None, :]
    sin = jnp.sin(ang)[:, None, :]
    x32 = x.astype(jnp.float32)
    x1 = x32[..., :half]
    x2 = x32[..., half:ROT_DIM]
    out = jnp.concatenate([x1 * cos - x2 * sin, x1 * sin + x2 * cos, x32[..., ROT_DIM:]], axis=-1)
    return out.astype(x.dtype)


def _band_attention(q, k, v, nwin):
    n, l, h, d = q.shape
    blk = nwin
    nb = -(-l // blk)
    lp = nb * blk
    qb = jnp.pad(q, ((0, 0), (0, lp - l), (0, 0), (0, 0))).reshape(n, nb, blk, h, d)
    kpad = ((0, 0), (blk, lp - l), (0, 0), (0, 0))
    kb = jnp.pad(k, kpad).reshape(n, nb + 1, blk, h, d)
    vb = jnp.pad(v, kpad).reshape(n, nb + 1, blk, h, d)
    kband = jnp.concatenate([kb[:, :-1], kb[:, 1:]], axis=2)
    vband = jnp.concatenate([vb[:, :-1], vb[:, 1:]], axis=2)
    s = jnp.einsum('nbqhd,nbkhd->nbhqk', qb, kband, preferred_element_type=jnp.float32) * (d ** -0.5)
    qi = jnp.arange(blk)[:, None]
    ki = jnp.arange(2 * blk)[None, :]
    rel = qi + blk - ki
    j_abs = jnp.arange(nb)[:, None, None] * blk + ki[None] - blk
    mask = (rel >= 0)[None] & (rel <= nwin)[None] & (j_abs >= 0)
    s = jnp.where(mask[None, :, None], s, NEG_BIG)
    m = jnp.max(s, axis=-1, keepdims=True)
    p = jnp.exp(s - m)
    den = jnp.sum(p, axis=-1)
    o = jnp.einsum('nbhqk,nbkhd->nbqhd', p, vband.astype(jnp.float32))
    o = (o / den.transpose(0, 1, 3, 2)[..., None]).reshape(n, lp, h, d)[:, :l]
    lse = (m[..., 0] + jnp.log(den)).transpose(0, 1, 3, 2).reshape(n, lp, h)[:, :l]
    return o, lse


def _dilated_prompt(q, k, v, dil, nwin):
    n, t, h, d = q.shape
    ls = t // dil

    def sub(a):
        return a.reshape(n, ls, dil, h, d).transpose(0, 2, 1, 3, 4).reshape(n * dil, ls, h, d)

    o, lse = _band_attention(sub(q), sub(k), sub(v), nwin)
    o = o.reshape(n, dil, ls, h, d).transpose(0, 2, 1, 3, 4).reshape(n, t, h, d)
    lse = lse.reshape(n, dil, ls, h).transpose(0, 2, 1, 3).reshape(n, t, h)
    return o, lse


def _dilated_sample(q, k, v, buf, dil, nwin):
    n, t, h, d = q.shape
    nbuf = buf.shape[1]
    kc = jnp.concatenate([buf[:, :, 0].astype(k.dtype), k], axis=1)
    vc = jnp.concatenate([buf[:, :, 1].astype(v.dtype), v], axis=1)
    idx = nbuf + jnp.arange(t)[:, None] - dil * jnp.arange(nwin + 1)[None, :]
    valid = idx >= 0
    idx = jnp.maximum(idx, 0)
    kg = kc[:, idx]
    vg = vc[:, idx]
    s = jnp.einsum('nthd,ntkhd->nhtk', q, kg, preferred_element_type=jnp.float32) * (d ** -0.5)
    s = jnp.where(valid[None, None], s, NEG_BIG)
    m = jnp.max(s, axis=-1, keepdims=True)
    p = jnp.exp(s - m)
    den = jnp.sum(p, axis=-1)
    o = jnp.einsum('nhtk,ntkhd->nthd', p, vg.astype(jnp.float32)) / den.transpose(0, 2, 1)[..., None]
    lse = (m[..., 0] + jnp.log(den)).transpose(0, 2, 1)
    new_buf = jnp.stack([kc, vc], axis=2)[:, kc.shape[1] - nbuf:]
    return o, lse, new_buf


def _mixer_a(h, pos, w_qkv, w_o, bufs):
    n, t, _ = h.shape
    qkv = (h @ w_qkv).reshape(n, t, N_GROUPS_A, 3, HEADS_PER_GROUP, HEAD_DIM)
    outs, lses, new_bufs = [], [], []
    for g in range(N_GROUPS_A):
        dil = DILATIONS[g]
        nwin = WINDOWS[g] // dil
        q = _rope(qkv[:, :, g, 0], pos)
        k = _rope(qkv[:, :, g, 1], pos)
        v = qkv[:, :, g, 2]
        if bufs is None:
            o, lse = _dilated_prompt(q, k, v, dil, nwin)
            keep = min(WINDOWS[g], t)
            nbuf = jnp.stack([k, v], axis=2)[:, t - keep:]
        else:
            o, lse, nbuf = _dilated_sample(q, k, v, bufs[g], dil, nwin)
        outs.append(o)
        lses.append(lse)
        new_bufs.append(nbuf)
    wgt = jax.nn.softmax(jnp.stack(lses, axis=0), axis=0)
    o = outs[0] * wgt[0][..., None]
    for g in range(1, N_GROUPS_A):
        o = o + outs[g] * wgt[g][..., None]
    y = o.reshape(n, t, A_WIDTH).astype(h.dtype) @ w_o
    return y, new_bufs


def _sb_rows(q, k, v, bias, mask, carry_sp):
    s = jnp.einsum('nthd,nshd->nhts', q, k, preferred_element_type=jnp.float32) * (HEAD_DIM ** -0.5)
    s = s + bias.astype(jnp.float32)[None, :, None, None]
    sp = jnp.where(mask, jax.nn.softplus(s), 0.0)
    right = lax.cumsum(sp, axis=3, reverse=True) - sp + carry_sp[..., None]
    a = jnp.where(mask, jnp.exp(jax.nn.log_sigmoid(s) - right), 0.0)
    o = jnp.einsum('nhts,nshd->nthd', a, v.astype(jnp.float32))
    return o, carry_sp + jnp.sum(sp, axis=3)


def _sb_prompt(q, k, v, bias):
    n, t, h, d = q.shape
    blk = min(SB_BLOCK, t)
    nb = t // blk
    qb = q.reshape(n, nb, blk, h, d).transpose(1, 0, 2, 3, 4)
    key_pos = jnp.arange(t)

    def one(args):
        qi, b = args
        mask = key_pos[None, :] < (b * blk + jnp.arange(blk))[:, None]
        o, _ = _sb_rows(qi, k, v, bias, mask, jnp.zeros((n, h, blk), jnp.float32))
        return o

    o = lax.map(one, (qb, jnp.arange(nb)))
    return o.transpose(1, 0, 2, 3, 4).reshape(n, t, h, d)


def _sb_sample(q, k, v, bias, kv_pages, page_table):
    n, t, h, d = q.shape
    pos = jnp.arange(t)
    o, sp_acc = _sb_rows(q, k, v, bias, pos[None, :] < pos[:, None], jnp.zeros((n, h, t), jnp.float32))
    page_mask = jnp.ones((t, kv_pages.shape[1]), dtype=bool)

    def step(carry, phys):
        acc, spc = carry
        kv = kv_pages[phys]
        o_p, spc = _sb_rows(q, kv[:, :, 0], kv[:, :, 1], bias, page_mask, spc)
        return (acc + o_p, spc), None

    (o, _), _ = lax.scan(step, (o, sp_acc), page_table.T[::-1])
    return o


def _conv_ffn(h, prev, w_up, w_conv, b_conv, w_down):
    t = h.shape[1]
    u = h @ w_up
    ext = jnp.concatenate([prev.astype(u.dtype), u], axis=1)
    y = b_conv
    for kk in range(CONV_W):
        y = y + w_conv[kk] * ext[:, kk:kk + t]
    gate, up = jnp.split(y, 2, axis=-1)
    out = (jax.nn.silu(gate) * up) @ w_down
    return out, ext[:, t:]


def _forward(x, c, pos, win_bufs, kv_pages, page_table, conv_state, params):
    (norm_attn_g, norm_ffn_g, w_ada, b_ada, w_qkv_a, w_o_a, w_q_b, w_o_b, sb_bias,
     norm_kv_g, w_ada_kv, b_ada_kv, w_kv, w_up, w_conv, b_conv, w_down, norm_final_g) = params
    sample = kv_pages is not None
    n, t, _ = x.shape
    new_win = [[] for _ in range(N_GROUPS_A)]
    new_conv = []
    kv = None
    for l in range(DEPTH):
        sh1, sc1, g1, sh2, sc2, g2 = _ada(c, w_ada[l], b_ada[l], 6)
        h = _rms(x, norm_attn_g[l]) * (1 + sc1) + sh1
        if l < N_A_LAYERS:
            bufs = [win_bufs[g][l] for g in range(N_GROUPS_A)] if sample else None
            y, nbufs = _mixer_a(h, pos, w_qkv_a[l], w_o_a[l], bufs)
            for g in range(N_GROUPS_A):
                new_win[g].append(nbufs[g])
        else:
            j = l - N_A_LAYERS
            q = (h @ w_q_b[j]).reshape(n, t, N_HEADS_B, HEAD_DIM)
            if sample:
                o = _sb_sample(q, kv[:, :, 0], kv[:, :, 1], sb_bias[j], kv_pages, page_table)
            else:
                o = _sb_prompt(q, kv[:, :, 0], kv[:, :, 1], sb_bias[j])
            y = o.reshape(n, t, B_WIDTH).astype(x.dtype) @ w_o_b[j]
        x = x + g1 * y
        h = _rms(x, norm_ffn_g[l]) * (1 + sc2) + sh2
        prev = conv_state[l] if sample else jnp.zeros((n, CONV_W - 1, 2 * D_FF), x.dtype)
        y, st = _conv_ffn(h, prev, w_up[l], w_conv[l], b_conv[l], w_down[l])
        new_conv.append(st)
        x = x + g2 * y
        if l == N_A_LAYERS - 1:
            sh_kv, sc_kv = _ada(c, w_ada_kv, b_ada_kv, 2)
            h_kv = _rms(x, norm_kv_g) * (1 + sc_kv) + sh_kv
            kv = (h_kv @ w_kv).reshape(n, t, 2, N_HEADS_B, HEAD_DIM)
    y = _rms(x, norm_final_g)
    return y, [jnp.stack(b, axis=0) for b in new_win], kv, jnp.stack(new_conv, axis=0)


def setup_inputs(seed: int = 0) -> dict:
    key = jax.random.key(seed)
    ks = jax.random.split(key, 32)
    f32 = jnp.float32

    def nrm(k, shape, scale=1.0):
        return jax.random.normal(k, shape, f32) * scale

    d = D_MODEL
    n_pages = PAST_LEN // PAGE_SIZE
    used = DEC_BATCH * n_pages
    n_pool = used + max(1, used // 4)
    page_table = jax.random.permutation(ks[0], n_pool)[:used].reshape(DEC_BATCH, n_pages).astype(jnp.int32)
    ada_scale = 0.5 * d ** -0.5
    return {
        'x_prompt': nrm(ks[1], (BATCH, SEQ, d)),
        'x_sample': nrm(ks[2], (DEC_BATCH, DEC_SEQ, d)),
        'cache_win0': nrm(ks[3], (N_A_LAYERS, DEC_BATCH, min(WINDOWS[0], PAST_LEN), 2, HEADS_PER_GROUP, HEAD_DIM)),
        'cache_win1': nrm(ks[4], (N_A_LAYERS, DEC_BATCH, min(WINDOWS[1], PAST_LEN), 2, HEADS_PER_GROUP, HEAD_DIM)),
        'cache_win2': nrm(ks[5], (N_A_LAYERS, DEC_BATCH, min(WINDOWS[2], PAST_LEN), 2, HEADS_PER_GROUP, HEAD_DIM)),
        'cache_kv_pages': nrm(ks[6], (n_pool, PAGE_SIZE, 2, N_HEADS_B, HEAD_DIM)),
        'state_conv': nrm(ks[7], (DEPTH, DEC_BATCH, CONV_W - 1, 2 * D_FF)),
        'page_table': page_table,
        'c_prompt': nrm(ks[8], (BATCH, d)),
        'c_sample': nrm(ks[9], (DEC_BATCH, d)),
        'norm_attn_g': 1.0 + nrm(ks[10], (DEPTH, d), 0.1),
        'norm_ffn_g': 1.0 + nrm(ks[11], (DEPTH, d), 0.1),
        'w_ada': nrm(ks[12], (DEPTH, d, 6 * d), ada_scale),
        'b_ada': nrm(ks[13], (DEPTH, 6 * d), 0.01),
        'w_qkv_a': nrm(ks[14], (N_A_LAYERS, d, QKV_A_WIDTH), d ** -0.5),
        'w_o_a': nrm(ks[15], (N_A_LAYERS, A_WIDTH, d), A_WIDTH ** -0.5),
        'w_q_b': nrm(ks[16], (N_B_LAYERS, d, B_WIDTH), d ** -0.5),
        'w_o_b': nrm(ks[17], (N_B_LAYERS, B_WIDTH, d), B_WIDTH ** -0.5),
        'sb_bias': SB_BIAS_INIT + nrm(ks[27], (N_B_LAYERS, N_HEADS_B), 0.1),
        'norm_kv_g': 1.0 + nrm(ks[18], (d,), 0.1),
        'w_ada_kv': nrm(ks[19], (d, 2 * d), ada_scale),
        'b_ada_kv': nrm(ks[20], (2 * d,), 0.01),
        'w_kv': nrm(ks[21], (d, 2 * B_WIDTH), d ** -0.5),
        'w_up': nrm(ks[22], (DEPTH, d, 2 * D_FF), d ** -0.5),
        'w_conv': nrm(ks[23], (DEPTH, CONV_W, 2 * D_FF), CONV_W ** -0.5),
        'b_conv': nrm(ks[24], (DEPTH, 2 * D_FF), 0.01),
        'w_down': nrm(ks[25], (DEPTH, D_FF, d), D_FF ** -0.5),
        'norm_final_g': 1.0 + nrm(ks[26], (d,), 0.1),
    }


def reference(x_prompt, x_sample, cache_win0, cache_win1, cache_win2, cache_kv_pages, state_conv, page_table,
              c_prompt, c_sample, norm_attn_g, norm_ffn_g, w_ada, b_ada, w_qkv_a, w_o_a, w_q_b, w_o_b, sb_bias,
              norm_kv_g, w_ada_kv, b_ada_kv, w_kv, w_up, w_conv, b_conv, w_down, norm_final_g):
    params = (norm_attn_g, norm_ffn_g, w_ada, b_ada, w_qkv_a, w_o_a, w_q_b, w_o_b, sb_bias,
              norm_kv_g, w_ada_kv, b_ada_kv, w_kv, w_up, w_conv, b_conv, w_down, norm_final_g)
    pos_p = jnp.arange(x_prompt.shape[1])
    pos_s = PAST_LEN + jnp.arange(x_sample.shape[1])
    y_prompt, win_p, kv_p, conv_p = _forward(x_prompt, c_prompt, pos_p, None, None, None, None, params)
    y_sample, win_s, kv_s, conv_s = _forward(x_sample, c_sample, pos_s, (cache_win0, cache_win1, cache_win2),
                                             cache_kv_pages, page_table, state_conv, params)
    return (y_prompt, y_sample, win_p[0], win_s[0], win_p[1], win_s[1], win_p[2], win_s[2],
            kv_p, kv_s, conv_p, conv_s)
```

```python
import functools

import jax
import jax.numpy as jnp
from jax import lax
from jax.experimental import pallas as pl
from jax.experimental.pallas import tpu as pltpu

F32 = jnp.float32
BF16 = jnp.bfloat16

HEAD_DIM = 128
N_GROUPS_A = 3
WINDOWS = (128, 512, 2048)
DILATIONS = (1, 4, 16)
ROT_DIM = HEAD_DIM // 4
ROPE_THETA = 500000.0
CONV_W = 3
RMS_EPS = 1e-6
NEG_BIG = -1e30
BAND = 128
SUBLANES = 8
ADA_ROWS = 16
VMEM_BIG = 56 * 1024 * 1024
VMEM_MID = 48 * 1024 * 1024


def _params(sem, vmem=VMEM_MID):
    return pltpu.CompilerParams(dimension_semantics=sem, vmem_limit_bytes=vmem)


def _softplus(x):
    return jnp.maximum(x, 0.0) + jnp.log1p(jnp.exp(-jnp.abs(x)))


def _sigmoid(x):
    return 1.0 / (1.0 + jnp.exp(-x))


def _norm_kernel(x_ref, g_ref, sc_ref, sh_ref, o_ref):
    x = x_ref[...]
    ms = jnp.mean(x * x, axis=-1, keepdims=True)
    y = x * lax.rsqrt(ms + RMS_EPS) * g_ref[...]
    o_ref[...] = (y * (1.0 + sc_ref[...]) + sh_ref[...]).astype(o_ref.dtype)


def _final_norm_kernel(x_ref, g_ref, o_ref):
    x = x_ref[...]
    ms = jnp.mean(x * x, axis=-1, keepdims=True)
    o_ref[...] = (x * lax.rsqrt(ms + RMS_EPS) * g_ref[...]).astype(o_ref.dtype)


def _norm_mod(x, g, sc, sh, *, bm, rows_per_mod):
    m, d = x.shape
    r = sc.shape[1]
    mod_spec = pl.BlockSpec((None, r, d), lambda i: ((i * bm) // rows_per_mod, 0, 0))
    return pl.pallas_call(
        _norm_kernel,
        grid=(m // bm,),
        in_specs=[pl.BlockSpec((bm, d), lambda i: (i, 0)),
                  pl.BlockSpec((1, d), lambda i: (0, 0)),
                  mod_spec, mod_spec],
        out_specs=pl.BlockSpec((bm, d), lambda i: (i, 0)),
        out_shape=jax.ShapeDtypeStruct((m, d), BF16),
        compiler_params=_params(("arbitrary",)),
        name="norm_mod",
    )(x, g, sc, sh)


def _final_norm(x, g, *, bm):
    m, d = x.shape
    return pl.pallas_call(
        _final_norm_kernel,
        grid=(m // bm,),
        in_specs=[pl.BlockSpec((bm, d), lambda i: (i, 0)),
                  pl.BlockSpec((1, d), lambda i: (0, 0))],
        out_specs=pl.BlockSpec((bm, d), lambda i: (i, 0)),
        out_shape=jax.ShapeDtypeStruct((m, d), F32),
        compiler_params=_params(("arbitrary",)),
        name="final_norm",
    )(x, g)


def _cast_weight(w_ref, wb_ref):
    k = w_ref.shape[0]
    chunk = 512 if k % 512 == 0 else k

    def body(c, _):
        r0 = pl.multiple_of(c * chunk, chunk)
        wb_ref[pl.ds(r0, chunk), :] = w_ref[pl.ds(r0, chunk), :].astype(BF16)
        return 0

    lax.fori_loop(0, k // chunk, body, 0)


def _rope_tile(acc, tab_ref):
    cos = tab_ref[:, 0:HEAD_DIM]
    s_lo = tab_ref[:, HEAD_DIM:2 * HEAD_DIM]
    s_hi = tab_ref[:, 2 * HEAD_DIM:3 * HEAD_DIM]
    half = ROT_DIM // 2
    pieces = []
    for hh in range(acc.shape[1] // HEAD_DIM):
        x = acc[:, hh * HEAD_DIM:(hh + 1) * HEAD_DIM]
        nxt = pltpu.roll(x, HEAD_DIM - half, axis=1)
        prv = pltpu.roll(x, half, axis=1)
        pieces.append(x * cos + nxt * s_lo + prv * s_hi)
    return pieces


def _mm_kernel(*refs, mode, silu_in, qkv_width):
    if mode == "plain":
        x_ref, w_ref, o_ref, wb_ref = refs
    elif mode == "bias":
        x_ref, w_ref, b_ref, o_ref, wb_ref = refs
    elif mode == "resid":
        x_ref, w_ref, res_ref, gate_ref, o_ref, wb_ref = refs
    elif mode == "rope":
        x_ref, w_ref, tab_ref, o_ref, wb_ref = refs
    else:
        raise ValueError(mode)

    @pl.when(pl.program_id(1) == 0)
    def _():
        _cast_weight(w_ref, wb_ref)

    x = x_ref[...]
    if silu_in:
        x = x * _sigmoid(x)
    acc = jnp.dot(x.astype(BF16), wb_ref[...], preferred_element_type=F32)
    if mode == "plain":
        o_ref[...] = acc.astype(o_ref.dtype)
    elif mode == "bias":
        o_ref[...] = (acc + b_ref[...]).astype(o_ref.dtype)
    elif mode == "resid":
        o_ref[...] = (res_ref[...] + gate_ref[...] * acc).astype(o_ref.dtype)
    else:
        bn = acc.shape[1]
        kind = ((pl.program_id(0) * bn) // qkv_width) % 3

        @pl.when(kind == 2)
        def _():
            o_ref[...] = acc.astype(o_ref.dtype)

        @pl.when(kind != 2)
        def _():
            for hh, piece in enumerate(_rope_tile(acc, tab_ref)):
                o_ref[:, hh * HEAD_DIM:(hh + 1) * HEAD_DIM] = piece.astype(o_ref.dtype)


def _matmul(x, w, layer, *, bm, bn, out_dtype, mode="plain", extras=(), extra_specs=(),
            silu_in=False, qkv_width=0, vmem=VMEM_BIG, name="matmul"):
    m, k = x.shape
    n = w.shape[2]
    kern = functools.partial(_mm_kernel, mode=mode, silu_in=silu_in, qkv_width=qkv_width)
    return pl.pallas_call(
        kern,
        grid=(n // bn, m // bm),
        in_specs=[pl.BlockSpec((bm, k), lambda j, i: (i, 0)),
                  pl.BlockSpec((None, k, bn), lambda j, i: (layer, 0, j)),
                  *extra_specs],
        out_specs=pl.BlockSpec((bm, bn), lambda j, i: (i, j)),
        out_shape=jax.ShapeDtypeStruct((m, n), out_dtype),
        scratch_shapes=[pltpu.VMEM((k, bn), BF16)],
        compiler_params=_params(("arbitrary", "arbitrary"), vmem),
        name=name,
    )(x, w, *extras)


def _mm_ktiled_kernel(x_ref, w_ref, res_ref, gate_ref, o_ref, acc_ref):
    kk = pl.program_id(2)

    @pl.when(kk == 0)
    def _():
        acc_ref[...] = jnp.zeros_like(acc_ref)

    acc_ref[...] += jnp.dot(x_ref[...], w_ref[...].astype(BF16), preferred_element_type=F32)

    @pl.when(kk == pl.num_programs(2) - 1)
    def _():
        o_ref[...] = res_ref[...] + gate_ref[...] * acc_ref[...]


def _matmul_ktiled_resid(x, w, layer, res, gate, *, bm, bn, bk, rows_per_mod, name="down"):
    m, k = x.shape
    n = w.shape[2]
    r = gate.shape[1]
    return pl.pallas_call(
        _mm_ktiled_kernel,
        grid=(n // bn, m // bm, k // bk),
        in_specs=[pl.BlockSpec((bm, bk), lambda j, i, kk: (i, kk)),
                  pl.BlockSpec((None, bk, bn), lambda j, i, kk: (layer, kk, j)),
                  pl.BlockSpec((bm, bn), lambda j, i, kk: (i, j)),
                  pl.BlockSpec((None, r, bn), lambda j, i, kk: ((i * bm) // rows_per_mod, 0, j))],
        out_specs=pl.BlockSpec((bm, bn), lambda j, i, kk: (i, j)),
        out_shape=jax.ShapeDtypeStruct((m, n), F32),
        scratch_shapes=[pltpu.VMEM((bm, bn), F32)],
        compiler_params=_params(("arbitrary", "arbitrary", "arbitrary"), VMEM_MID),
        name=name,
    )(x, w, res, gate)


def _ffn_up_kernel(*refs, bm, halo, tap, blocks_per_seq, has_prev):
    if has_prev:
        (h_ref, wg_ref, wu_ref, cwg_ref, cwu_ref, cbg_ref, cbu_ref, pg_ref, pu_ref,
         act_ref, stg_ref, stu_ref, wgb, wub, ug, uu) = refs
    else:
        (h_ref, wg_ref, wu_ref, cwg_ref, cwu_ref, cbg_ref, cbu_ref,
         act_ref, stg_ref, stu_ref, wgb, wub, ug, uu) = refs
    i = pl.program_id(1)

    @pl.when(i == 0)
    def _():
        _cast_weight(wg_ref, wgb)
        _cast_weight(wu_ref, wub)

    @pl.when(i % blocks_per_seq == 0)
    def _():
        if has_prev:
            ug[0:halo, :] = pg_ref[...]
            uu[0:halo, :] = pu_ref[...]
        else:
            ug[0:halo, :] = jnp.zeros((halo, ug.shape[1]), F32)
            uu[0:halo, :] = jnp.zeros((halo, uu.shape[1]), F32)

    h = h_ref[...]
    ug[halo:halo + bm, :] = jnp.dot(h, wgb[...], preferred_element_type=F32)
    uu[halo:halo + bm, :] = jnp.dot(h, wub[...], preferred_element_type=F32)

    def conv(u, cw_ref, cb_ref):
        y = cb_ref[...] + cw_ref[0:1, :] * u[halo - 2 * tap:halo - 2 * tap + bm, :]
        y = y + cw_ref[1:2, :] * u[halo - tap:halo - tap + bm, :]
        return y + cw_ref[2:3, :] * u[halo:halo + bm, :]

    gate = conv(ug, cwg_ref, cbg_ref)
    up = conv(uu, cwu_ref, cbu_ref)
    act_ref[...] = (gate * _sigmoid(gate) * up).astype(act_ref.dtype)
    tail_g = ug[bm:bm + halo, :]
    tail_u = uu[bm:bm + halo, :]
    stg_ref[...] = tail_g
    stu_ref[...] = tail_u
    ug[0:halo, :] = tail_g
    uu[0:halo, :] = tail_u


def _ffn_up(h, w_up, w_conv, b_conv3, layer, *, bm, bn, tap, rows_per_seq, prev=None):
    m, d = h.shape
    d_ff = w_up.shape[2] // 2
    nj = d_ff // bn
    halo = max(SUBLANES, 2 * tap)
    has_prev = prev is not None
    kern = functools.partial(_ffn_up_kernel, bm=bm, halo=halo, tap=tap,
                             blocks_per_seq=rows_per_seq // bm, has_prev=has_prev)
    in_specs = [pl.BlockSpec((bm, d), lambda j, i: (i, 0)),
                pl.BlockSpec((None, d, bn), lambda j, i: (layer, 0, j)),
                pl.BlockSpec((None, d, bn), lambda j, i: (layer, 0, j + nj)),
                pl.BlockSpec((None, CONV_W, bn), lambda j, i: (layer, 0, j)),
                pl.BlockSpec((None, CONV_W, bn), lambda j, i: (layer, 0, j + nj)),
                pl.BlockSpec((None, 1, bn), lambda j, i: (layer, 0, j)),
                pl.BlockSpec((None, 1, bn), lambda j, i: (layer, 0, j + nj))]
    args = [h, w_up, w_up, w_conv, w_conv, b_conv3, b_conv3]
    if has_prev:
        in_specs += [pl.BlockSpec((halo, bn), lambda j, i: (0, j)),
                     pl.BlockSpec((halo, bn), lambda j, i: (0, j + nj))]
        args += [prev, prev]
    st_spec = pl.BlockSpec((None, halo, bn), lambda j, i: (i, 0, j))
    return pl.pallas_call(
        kern,
        grid=(nj, m // bm),
        in_specs=in_specs,
        out_specs=[pl.BlockSpec((bm, bn), lambda j, i: (i, j)), st_spec, st_spec],
        out_shape=[jax.ShapeDtypeStruct((m, d_ff), BF16),
                   jax.ShapeDtypeStruct((m // bm, halo, d_ff), F32),
                   jax.ShapeDtypeStruct((m // bm, halo, d_ff), F32)],
        scratch_shapes=[pltpu.VMEM((d, bn), BF16), pltpu.VMEM((d, bn), BF16),
                        pltpu.VMEM((bm + halo, bn), F32), pltpu.VMEM((bm + halo, bn), F32)],
        compiler_params=_params(("arbitrary", "arbitrary"), VMEM_BIG),
        name="ffn_up",
    )(*args)


def _attn_a_prompt_kernel(q0, k0, v0, q1, k1, v1, q2, k2, v2, o_ref, og, lg, *, seq):
    qkv = ((q0, k0, v0), (q1, k1, v1), (q2, k2, v2))
    scale = HEAD_DIM ** -0.5
    row = lax.broadcasted_iota(jnp.int32, (BAND, BAND), 0)
    col = lax.broadcasted_iota(jnp.int32, (BAND, BAND), 1)
    cur_ok = col <= row
    prev_ok = col >= row

    for g in range(N_GROUPS_A):
        dil = DILATIONS[g]
        nb = seq // dil // BAND
        q_ref, k_ref, v_ref = qkv[g]

        def rows(start, dil=dil):
            if dil == 1:
                return pl.ds(pl.multiple_of(start, BAND), BAND)
            return pl.ds(start, BAND, stride=dil)

        def body(idx, _, g=g, dil=dil, nb=nb, q_ref=q_ref, k_ref=k_ref, v_ref=v_ref, rows=rows):
            res = idx // nb
            blk = idx % nb
            start = res + blk * (BAND * dil)
            has_prev = blk > 0
            pstart = jnp.where(has_prev, start - BAND * dil, start)
            q = q_ref[rows(start), :].astype(BF16)
            kc = k_ref[rows(start), :].astype(BF16)
            vc = v_ref[rows(start), :].astype(BF16)
            kp = k_ref[rows(pstart), :].astype(BF16)
            vp = v_ref[rows(pstart), :].astype(BF16)
            dn = (((1,), (1,)), ((), ()))
            sc = lax.dot_general(q, kc, dn, preferred_element_type=F32) * scale
            sp = lax.dot_general(q, kp, dn, preferred_element_type=F32) * scale
            sc = jnp.where(cur_ok, sc, NEG_BIG)
            sp = jnp.where(jnp.logical_and(prev_ok, has_prev), sp, NEG_BIG)
            mx = jnp.maximum(jnp.max(sc, axis=1, keepdims=True), jnp.max(sp, axis=1, keepdims=True))
            pc = jnp.exp(sc - mx)
            pp = jnp.exp(sp - mx)
            den = jnp.sum(pc, axis=1, keepdims=True) + jnp.sum(pp, axis=1, keepdims=True)
            o = jnp.dot(pc.astype(BF16), vc, preferred_element_type=F32)
            o = o + jnp.dot(pp.astype(BF16), vp, preferred_element_type=F32)
            og[g, rows(start), :] = o / den
            lg[g, rows(start), :] = jnp.broadcast_to(mx + jnp.log(den), (BAND, HEAD_DIM))
            return 0

        lax.fori_loop(0, dil * nb, body, 0)

    chunk = 256

    def comb(c, _):
        r0 = pl.multiple_of(c * chunk, chunk)
        l0 = lg[0, pl.ds(r0, chunk), :]
        l1 = lg[1, pl.ds(r0, chunk), :]
        l2 = lg[2, pl.ds(r0, chunk), :]
        mx = jnp.maximum(jnp.maximum(l0, l1), l2)
        e0 = jnp.exp(l0 - mx)
        e1 = jnp.exp(l1 - mx)
        e2 = jnp.exp(l2 - mx)
        tot = e0 + e1 + e2
        o = (og[0, pl.ds(r0, chunk), :] * (e0 / tot) + og[1, pl.ds(r0, chunk), :] * (e1 / tot)
             + og[2, pl.ds(r0, chunk), :] * (e2 / tot))
        o_ref[pl.ds(r0, chunk), :] = o.astype(o_ref.dtype)
        return 0

    lax.fori_loop(0, seq // chunk, comb, 0)


def _attn_a_prompt(qkv, *, batch, seq, heads):
    m = qkv.shape[0]
    specs = []
    for g in range(N_GROUPS_A):
        for t in range(3):
            specs.append(pl.BlockSpec((seq, HEAD_DIM),
                                      lambda n, h, g=g, t=t: (n, (g * 3 + t) * heads + h)))
    return pl.pallas_call(
        functools.partial(_attn_a_prompt_kernel, seq=seq),
        grid=(batch, heads),
        in_specs=specs,
        out_specs=pl.BlockSpec((seq, HEAD_DIM), lambda n, h: (n, h)),
        out_shape=jax.ShapeDtypeStruct((m, heads * HEAD_DIM), BF16),
        scratch_shapes=[pltpu.VMEM((N_GROUPS_A, seq, HEAD_DIM), F32),
                        pltpu.VMEM((N_GROUPS_A, seq, HEAD_DIM), F32)],
        compiler_params=_params(("arbitrary", "arbitrary")),
        name="attn_a_prompt",
    )(*([qkv] * 9))


def _attn_a_sample_kernel(qn_ref, c0_ref, c1_ref, c2_ref, o_ref, *, heads, dec_seq):
    width = heads * HEAD_DIM
    kvw = 2 * width
    scale = HEAD_DIM ** -0.5
    caches = (c0_ref, c1_ref, c2_ref)
    hrow = lax.broadcasted_iota(jnp.int32, (heads, width), 0)
    hcol = lax.broadcasted_iota(jnp.int32, (heads, width), 1) // HEAD_DIM
    head_mask = hrow == hcol
    key_idx = lax.broadcasted_iota(jnp.int32, (heads, BAND), 1)
    dn = (((1,), (1,)), ((), ()))

    for t in range(dec_seq):
        outs, lses = [], []
        for g in range(N_GROUPS_A):
            base = g * 3 * width
            q_row = qn_ref[t:t + 1, base:base + width]
            qbd = jnp.where(head_mask, jnp.broadcast_to(q_row, (heads, width)), 0.0)
            c_ref = caches[g]
            off = 0 if g == 0 else t * kvw
            kb = c_ref[:, off:off + width].astype(BF16)
            vb = c_ref[:, off + width:off + kvw].astype(BF16)
            s_buf = lax.dot_general(qbd.astype(BF16), kb, dn, preferred_element_type=F32) * scale
            if g == 0:
                s_buf = jnp.where(key_idx >= t, s_buf, NEG_BIG)
                new_rows = range(t + 1)
            else:
                new_rows = (t,)
            s_new = []
            for tn in new_rows:
                k_row = qn_ref[tn:tn + 1, base + width:base + 2 * width]
                s_new.append(jnp.sum(qbd * k_row, axis=1, keepdims=True) * scale)
            mx = jnp.max(s_buf, axis=1, keepdims=True)
            for s in s_new:
                mx = jnp.maximum(mx, s)
            p_buf = jnp.exp(s_buf - mx)
            den = jnp.sum(p_buf, axis=1, keepdims=True)
            o = jnp.dot(p_buf.astype(BF16), vb, preferred_element_type=F32)
            for tn, s in zip(new_rows, s_new):
                p = jnp.exp(s - mx)
                den = den + p
                o = o + p * qn_ref[tn:tn + 1, base + 2 * width:base + 3 * width]
            o = jnp.where(head_mask, o / den, 0.0)
            lse = jnp.where(head_mask, jnp.broadcast_to(mx + jnp.log(den), (heads, width)), 0.0)
            outs.append(jnp.sum(o, axis=0, keepdims=True))
            lses.append(jnp.sum(lse, axis=0, keepdims=True))
        mx = jnp.maximum(jnp.maximum(lses[0], lses[1]), lses[2])
        es = [jnp.exp(l - mx) for l in lses]
        tot = es[0] + es[1] + es[2]
        row = outs[0] * (es[0] / tot) + outs[1] * (es[1] / tot) + outs[2] * (es[2] / tot)
        o_ref[t:t + 1, :] = row.astype(o_ref.dtype)


def _attn_a_sample(qn, caches, layer, *, heads):
    db, dec_seq, _ = qn.shape
    width = heads * HEAD_DIM
    kvw = 2 * width
    views, specs = [], []
    for g in range(N_GROUPS_A):
        c = caches[g]
        nlay, _, nbuf = c.shape[:3]
        dil = DILATIONS[g]
        views.append(c.reshape(nlay, db, nbuf // dil, dil * kvw))
        cols = kvw if g == 0 else dec_seq * kvw
        specs.append(pl.BlockSpec((None, None, nbuf // dil, cols), lambda n: (layer, n, 0, 0)))
    return pl.pallas_call(
        functools.partial(_attn_a_sample_kernel, heads=heads, dec_seq=dec_seq),
        grid=(db,),
        in_specs=[pl.BlockSpec((None, dec_seq, qn.shape[2]), lambda n: (n, 0, 0)), *specs],
        out_specs=pl.BlockSpec((None, dec_seq, width), lambda n: (n, 0, 0)),
        out_shape=jax.ShapeDtypeStruct((db, dec_seq, width), F32),
        compiler_params=_params(("arbitrary",)),
        name="attn_a_sample",
    )(qn, *views)


def _sb_block(s, tri, carry, mask):
    sp = _softplus(s)
    if mask is not None:
        sp = jnp.where(mask, sp, 0.0)
    hi = sp.astype(BF16)
    lo = (sp - hi.astype(F32)).astype(BF16)
    right = (jnp.dot(hi, tri, preferred_element_type=F32)
             + jnp.dot(lo, tri, preferred_element_type=F32) + carry)
    a = jnp.exp(s - sp - right)
    if mask is not None:
        a = jnp.where(mask, a, 0.0)
    return a, carry + jnp.sum(sp, axis=1, keepdims=True)


def _sb_prompt_kernel(q_ref, k_ref, v_ref, b_ref, tri_ref, o_ref, kb_ref, vb_ref, *, seq, blk):
    scale = HEAD_DIM ** -0.5
    nblk = seq // blk
    chunk = 512 if seq % 512 == 0 else seq

    def cast(c, _):
        r0 = pl.multiple_of(c * chunk, chunk)
        kb_ref[pl.ds(r0, chunk), :] = k_ref[pl.ds(r0, chunk), :].astype(BF16)
        vb_ref[pl.ds(r0, chunk), :] = v_ref[pl.ds(r0, chunk), :].astype(BF16)
        return 0

    lax.fori_loop(0, seq // chunk, cast, 0)
    row = lax.broadcasted_iota(jnp.int32, (blk, blk), 0)
    col = lax.broadcasted_iota(jnp.int32, (blk, blk), 1)
    diag_mask = col < row
    dn = (((1,), (1,)), ((), ()))

    def q_block(qb, _):
        q0 = pl.multiple_of(qb * blk, blk)
        q = q_ref[pl.ds(q0, blk), :]

        def logits(k0):
            k = kb_ref[pl.ds(k0, blk), :]
            return lax.dot_general(q, k, dn, preferred_element_type=F32) * scale + b_ref[...]

        a, carry = _sb_block(logits(q0), tri_ref[...], jnp.zeros((blk, 1), F32), diag_mask)
        acc = jnp.dot(a.astype(BF16), vb_ref[pl.ds(q0, blk), :], preferred_element_type=F32)

        def k_block(it, state):
            acc, carry = state
            k0 = pl.multiple_of((qb - 1 - it) * blk, blk)
            a, carry = _sb_block(logits(k0), tri_ref[...], carry, None)
            acc = acc + jnp.dot(a.astype(BF16), vb_ref[pl.ds(k0, blk), :], preferred_element_type=F32)
            return acc, carry

        acc, _ = lax.fori_loop(0, qb, k_block, (acc, carry))
        o_ref[pl.ds(q0, blk), :] = acc.astype(o_ref.dtype)
        return 0

    lax.fori_loop(0, nblk, q_block, 0)


def _tri(blk):
    r = lax.broadcasted_iota(jnp.int32, (blk, blk), 0)
    c = lax.broadcasted_iota(jnp.int32, (blk, blk), 1)
    return (r > c).astype(BF16)


def _sb_prompt(q, kv, bias, *, batch, seq, heads, blk):
    m = q.shape[0]
    bias_b = jnp.broadcast_to(bias.astype(F32)[:, None, None], (heads, 1, blk))
    return pl.pallas_call(
        functools.partial(_sb_prompt_kernel, seq=seq, blk=blk),
        grid=(batch, heads),
        in_specs=[pl.BlockSpec((seq, HEAD_DIM), lambda n, h: (n, h)),
                  pl.BlockSpec((seq, HEAD_DIM), lambda n, h: (n, h)),
                  pl.BlockSpec((seq, HEAD_DIM), lambda n, h: (n, heads + h)),
                  pl.BlockSpec((None, 1, blk), lambda n, h: (h, 0, 0)),
                  pl.BlockSpec((blk, blk), lambda n, h: (0, 0))],
        out_specs=pl.BlockSpec((seq, HEAD_DIM), lambda n, h: (n, h)),
        out_shape=jax.ShapeDtypeStruct((m, heads * HEAD_DIM), BF16),
        scratch_shapes=[pltpu.VMEM((seq, HEAD_DIM), BF16), pltpu.VMEM((seq, HEAD_DIM), BF16)],
        compiler_params=_params(("arbitrary", "arbitrary")),
        name="sb_prompt",
    )(q, kv, kv, bias_b, _tri(blk))


def _sb_sample_kernel(pt_ref, q_ref, new_ref, page_ref, b_ref, tri_ref, o_ref,
                      qbd_ref, acc_ref, carry_ref, *, heads, dec_seq, page):
    del pt_ref
    width = heads * HEAD_DIM
    rows = dec_seq * heads
    p = pl.program_id(1)
    scale = HEAD_DIM ** -0.5
    hrow = lax.broadcasted_iota(jnp.int32, (heads, width), 0)
    hcol = lax.broadcasted_iota(jnp.int32, (heads, width), 1) // HEAD_DIM
    head_mask = hrow == hcol
    dn = (((1,), (1,)), ((), ()))

    def process(kv_ref, mask):
        k = kv_ref[:, 0:width].astype(BF16)
        v = kv_ref[:, width:2 * width].astype(BF16)
        s = lax.dot_general(qbd_ref[...], k, dn, preferred_element_type=F32) * scale + b_ref[...]
        a, carry = _sb_block(s, tri_ref[...], carry_ref[:, 0:1], mask)
        acc_ref[...] += jnp.dot(a.astype(BF16), v, preferred_element_type=F32)
        carry_ref[...] = jnp.broadcast_to(carry, carry_ref.shape)

    @pl.when(p == 0)
    def _():
        for t in range(dec_seq):
            q_row = jnp.broadcast_to(q_ref[t:t + 1, :], (heads, width))
            qbd_ref[t * heads:(t + 1) * heads, :] = jnp.where(head_mask, q_row, 0.0).astype(BF16)
        acc_ref[...] = jnp.zeros_like(acc_ref)
        carry_ref[...] = jnp.zeros_like(carry_ref)
        t_of_row = lax.broadcasted_iota(jnp.int32, (rows, page), 0) // heads
        key = lax.broadcasted_iota(jnp.int32, (rows, page), 1)
        process(new_ref, key < t_of_row)

    @pl.when(p > 0)
    def _():
        process(page_ref, None)

    @pl.when(p == pl.num_programs(1) - 1)
    def _():
        for t in range(dec_seq):
            o = jnp.where(head_mask, acc_ref[t * heads:(t + 1) * heads, :], 0.0)
            o_ref[t:t + 1, :] = jnp.sum(o, axis=0, keepdims=True).astype(o_ref.dtype)


def _sb_sample(q, kv_new_pad, pages, page_table, bias, *, heads):
    db, dec_seq, width = q.shape
    page = pages.shape[1]
    n_pages = page_table.shape[1]
    rows = dec_seq * heads
    bias_b = jnp.broadcast_to(jnp.tile(bias.astype(F32), dec_seq)[:, None], (rows, page))

    def page_map(n, p, pt):
        return (pt[n, n_pages - jnp.maximum(p, 1)], 0, 0)

    grid_spec = pltpu.PrefetchScalarGridSpec(
        num_scalar_prefetch=1,
        grid=(db, n_pages + 1),
        in_specs=[pl.BlockSpec((None, dec_seq, width), lambda n, p, pt: (n, 0, 0)),
                  pl.BlockSpec((None, page, 2 * width), lambda n, p, pt: (n, 0, 0)),
                  pl.BlockSpec((None, page, 2 * width), page_map),
                  pl.BlockSpec((rows, page), lambda n, p, pt: (0, 0)),
                  pl.BlockSpec((page, page), lambda n, p, pt: (0, 0))],
        out_specs=pl.BlockSpec((None, dec_seq, width), lambda n, p, pt: (n, 0, 0)),
        scratch_shapes=[pltpu.VMEM((rows, width), BF16), pltpu.VMEM((rows, width), F32),
                        pltpu.VMEM((rows, HEAD_DIM), F32)],
    )
    return pl.pallas_call(
        functools.partial(_sb_sample_kernel, heads=heads, dec_seq=dec_seq, page=page),
        grid_spec=grid_spec,
        out_shape=jax.ShapeDtypeStruct((db, dec_seq, width), F32),
        compiler_params=_params(("arbitrary", "arbitrary")),
        name="sb_sample",
    )(page_table, q, kv_new_pad, pages, bias_b, _tri(page))


def _rope_table(pos):
    half = ROT_DIM // 2
    inv = ROPE_THETA ** (-jnp.arange(half, dtype=F32) / half)
    ang = pos.astype(F32)[:, None] * inv[None, :]
    cos, sin = jnp.cos(ang), jnp.sin(ang)
    t = pos.shape[0]
    ones = jnp.ones((t, HEAD_DIM - ROT_DIM), F32)
    zeros = jnp.zeros((t, HEAD_DIM - ROT_DIM), F32)
    zh = jnp.zeros((t, half), F32)
    return jnp.concatenate([cos, cos, ones, -sin, zh, zeros, zh, sin, zeros], axis=1)


def _pick(v, cands):
    for c in cands:
        if v % c == 0:
            return c
    return v


def _trunk(x, mods, mod_kv, tabs, cfg, weights, sample):
    (norm_attn_g, norm_ffn_g, w_qkv_a, w_o_a, w_q_b, w_o_b, sb_bias, norm_kv_g, w_kv,
     w_up, w_conv, b_conv3, w_down, norm_final_g) = weights
    m, d = x.shape
    bm, bmn, rpm = cfg["bm"], cfg["bm_norm"], cfg["rows_per_mod"]
    depth = w_up.shape[0]
    n_a = w_qkv_a.shape[0]
    heads_a = w_o_a.shape[1] // HEAD_DIM
    heads_b = w_q_b.shape[2] // HEAD_DIM
    d_ff = w_down.shape[1]
    r = mods[0][0].shape[1]

    def mod_spec(bn):
        return pl.BlockSpec((None, r, bn), lambda j, i: ((i * bm) // rpm, 0, j))

    def tile(n, cands=(512, 256, 128)):
        return _pick(n, cands)

    qkv_all, tails = [], []
    kv = None
    for l in range(depth):
        sh1, sc1, g1, sh2, sc2, g2 = mods[l]
        h = _norm_mod(x, norm_attn_g[l][None], sc1, sh1, bm=bmn, rows_per_mod=rpm)
        if l < n_a:
            bn = tile(heads_a * HEAD_DIM)
            tab_spec = pl.BlockSpec((bm, 3 * HEAD_DIM), cfg["tab_map"])
            qkv = _matmul(h, w_qkv_a, l, bm=bm, bn=bn, out_dtype=F32, mode="rope",
                          extras=(tabs,), extra_specs=(tab_spec,),
                          qkv_width=heads_a * HEAD_DIM, name="qkv")
            qkv_all.append(qkv)
            o = cfg["attn_a"](qkv, l)
            w_o, lo = w_o_a, l
        else:
            j = l - n_a
            q = _matmul(h, w_q_b, j, bm=bm, bn=tile(w_q_b.shape[2]), out_dtype=cfg["q_dtype"], name="q_b")
            o = cfg["attn_b"](q, kv, sb_bias[j])
            w_o, lo = w_o_b, j
        bn = tile(d)
        x = _matmul(o, w_o, lo, bm=bm, bn=bn, out_dtype=F32, mode="resid",
                    extras=(x, g1), extra_specs=(pl.BlockSpec((bm, bn), lambda j, i: (i, j)), mod_spec(bn)),
                    name="attn_out")
        h = _norm_mod(x, norm_ffn_g[l][None], sc2, sh2, bm=bmn, rows_per_mod=rpm)
        act, tail_g, tail_u = _ffn_up(h, w_up, w_conv, b_conv3, l, bm=bm, bn=_pick(d_ff, (256, 128)),
                                      tap=cfg["tap"], rows_per_seq=cfg["rows_per_seq"],
                                      prev=sample["prev"][l] if sample else None)
        tails.append((tail_g, tail_u))
        x = _matmul_ktiled_resid(act, w_down, l, x, g2, bm=cfg["bm_down"], bn=_pick(d, (1024, 512, 256, 128)),
                                 bk=_pick(d_ff, (256, 128)), rows_per_mod=rpm)
        if l == n_a - 1:
            sh_kv, sc_kv = mod_kv
            h_kv = _norm_mod(x, norm_kv_g[None], sc_kv, sh_kv, bm=bmn, rows_per_mod=rpm)
            kv = _matmul(h_kv, w_kv[None], 0, bm=bm, bn=tile(w_kv.shape[1]), out_dtype=F32, name="kv")
    y = _final_norm(x, norm_final_g[None], bm=bmn)
    return y, qkv_all, kv, tails


def kernel(x_prompt, x_sample, cache_win0, cache_win1, cache_win2, cache_kv_pages, state_conv, page_table,
           c_prompt, c_sample, norm_attn_g, norm_ffn_g, w_ada, b_ada, w_qkv_a, w_o_a, w_q_b, w_o_b, sb_bias,
           norm_kv_g, w_ada_kv, b_ada_kv, w_kv, w_up, w_conv, b_conv, w_down, norm_final_g):
    batch, seq, d = x_prompt.shape
    db, dec_seq, _ = x_sample.shape
    depth = w_up.shape[0]
    n_a = w_qkv_a.shape[0]
    heads_a = w_o_a.shape[1] // HEAD_DIM
    heads_b = w_q_b.shape[2] // HEAD_DIM
    width_a = heads_a * HEAD_DIM
    width_b = heads_b * HEAD_DIM
    d_ff = w_down.shape[1]
    page = cache_kv_pages.shape[1]
    past_len = page_table.shape[1] * page
    caches = (cache_win0, cache_win1, cache_win2)
    assert seq % (BAND * DILATIONS[-1]) == 0 and batch + db <= ADA_ROWS and db == SUBLANES
    assert all(caches[g].shape[2] == WINDOWS[g] for g in range(N_GROUPS_A)) and dec_seq <= DILATIONS[1]

    c_all = jnp.concatenate([c_prompt, c_sample, jnp.zeros((ADA_ROWS - batch - db, d), F32)], axis=0)
    def ada(w, b, layer):
        n = w.shape[2]
        bn = _pick(n, (1024, 512, 256, 128))
        return _matmul(c_all, w, layer, bm=ADA_ROWS, bn=bn, out_dtype=F32, mode="bias", silu_in=True,
                       extras=(b.reshape(w.shape[0], 1, n),),
                       extra_specs=(pl.BlockSpec((None, 1, bn), lambda j, i: (layer, 0, j)),), name="ada")

    mod_all = [ada(w_ada, b_ada, l) for l in range(depth)]
    mod_kv_all = ada(w_ada_kv[None], b_ada_kv[None], 0)

    def prompt_mods(a, chunks):
        return tuple(c.reshape(batch, 1, d) for c in jnp.split(a[:batch], chunks, axis=-1))

    def sample_mods(a, chunks):
        return tuple(jnp.tile(c, (dec_seq, 1))[None] for c in jnp.split(a[batch:batch + db], chunks, axis=-1))

    b_conv3 = b_conv.reshape(depth, 1, 2 * d_ff)
    weights = (norm_attn_g, norm_ffn_g, w_qkv_a, w_o_a, w_q_b, w_o_b, sb_bias, norm_kv_g, w_kv,
               w_up, w_conv, b_conv3, w_down, norm_final_g)

    bm_p = _pick(seq, (1024, 512, 256, 128))
    sb_blk = _pick(seq, (256, 128))
    cfg_p = dict(
        bm=bm_p, bm_norm=_pick(seq, (256, 128)), bm_down=_pick(seq, (2048, 1024, 512, 256, 128)),
        rows_per_mod=seq, rows_per_seq=seq, tap=1, q_dtype=BF16,
        tab_map=lambda j, i: (i % (seq // bm_p), 0),
        attn_a=lambda qkv, l: _attn_a_prompt(qkv, batch=batch, seq=seq, heads=heads_a),
        attn_b=lambda q, kv, bias: _sb_prompt(q, kv, bias, batch=batch, seq=seq, heads=heads_b, blk=sb_blk),
    )
    y_p, qkv_p, kv_p, tails_p = _trunk(
        x_prompt.reshape(batch * seq, d),
        [prompt_mods(mod_all[l], 6) for l in range(depth)], prompt_mods(mod_kv_all, 2),
        _rope_table(jnp.arange(seq)), cfg_p, weights, None)

    win_p = []
    for g in range(N_GROUPS_A):
        keep = min(WINDOWS[g], seq)
        per_layer = [q.reshape(batch, seq, N_GROUPS_A, 3, heads_a, HEAD_DIM)[:, seq - keep:, g, 1:3]
                     for q in qkv_p]
        win_p.append(jnp.stack(per_layer, axis=0))
    kv_prompt = kv_p.reshape(batch, seq, 2, heads_b, HEAD_DIM)
    nblk = seq // bm_p
    conv_p = jnp.stack([
        jnp.concatenate([tg.reshape(batch, nblk, SUBLANES, d_ff)[:, -1, SUBLANES - (CONV_W - 1):],
                         tu.reshape(batch, nblk, SUBLANES, d_ff)[:, -1, SUBLANES - (CONV_W - 1):]], axis=-1)
        for tg, tu in tails_p], axis=0)

    m_s = dec_seq * db
    x_s = x_sample.transpose(1, 0, 2).reshape(m_s, d)
    pages = cache_kv_pages.reshape(cache_kv_pages.shape[0], page, 2 * width_b)
    prev = state_conv.transpose(0, 2, 1, 3).reshape(depth, (CONV_W - 1) * db, 2 * d_ff)

    def n_major(a):
        return a.reshape(dec_seq, db, a.shape[-1]).transpose(1, 0, 2)

    def t_major(a):
        return a.transpose(1, 0, 2).reshape(m_s, a.shape[-1])

    def attn_b_sample(q, kv, bias):
        kv_new = n_major(kv)
        kv_pad = jnp.concatenate([kv_new, jnp.zeros((db, page - dec_seq, 2 * width_b), F32)], axis=1)
        return t_major(_sb_sample(n_major(q), kv_pad, pages, page_table, bias, heads=heads_b))

    cfg_s = dict(
        bm=m_s, bm_norm=m_s, bm_down=m_s, rows_per_mod=m_s, rows_per_seq=m_s, tap=db, q_dtype=F32,
        tab_map=lambda j, i: (0, 0),
        attn_a=lambda qkv, l: t_major(_attn_a_sample(n_major(qkv), caches, l, heads=heads_a)),
        attn_b=attn_b_sample,
    )
    pos_s = past_len + jnp.repeat(jnp.arange(dec_seq), db)
    y_s, qkv_s, kv_s, tails_s = _trunk(
        x_s, [sample_mods(mod_all[l], 6) for l in range(depth)], sample_mods(mod_kv_all, 2),
        _rope_table(pos_s), cfg_s, weights, dict(prev=prev))

    win_s = []
    for g in range(N_GROUPS_A):
        per_layer = []
        for l in range(n_a):
            new = n_major(qkv_s[l]).reshape(db, dec_seq, N_GROUPS_A, 3, heads_a, HEAD_DIM)[:, :, g, 1:3]
            per_layer.append(jnp.concatenate([caches[g][l][:, dec_seq:], new], axis=1))
        win_s.append(jnp.stack(per_layer, axis=0))
    kv_sample = n_major(kv_s).reshape(db, dec_seq, 2, heads_b, HEAD_DIM)
    conv_s = jnp.stack([
        jnp.concatenate([tg[0], tu[0]], axis=-1).reshape(CONV_W - 1, db, 2 * d_ff).transpose(1, 0, 2)
        for tg, tu in tails_s], axis=0)

    y_prompt = y_p.reshape(batch, seq, d)
    y_sample = y_s.reshape(dec_seq, db, d).transpose(1, 0, 2)
    return (y_prompt, y_sample, win_p[0], win_s[0], win_p[1], win_s[1], win_p[2], win_s[2],
            kv_prompt, kv_sample, conv_p, conv_s)
```

```python
import functools

import jax
import jax.numpy as jnp
from jax import lax
from jax.experimental import pallas as pl
from jax.experimental.pallas import tpu as pltpu

F32 = jnp.float32
BF16 = jnp.bfloat16

HEAD_DIM = 128
N_GROUPS_A = 3
WINDOWS = (128, 512, 2048)
DILATIONS = (1, 4, 16)
ROT_DIM = HEAD_DIM // 4
ROPE_THETA = 500000.0
CONV_W = 3
RMS_EPS = 1e-6
NEG_BIG = -1e30
UNROLL_A = 4
FFN_SUB = 1024
K_BLOCK_BIG = 1024
BAND = 128
SUBLANES = 8
ADA_ROWS = 16
VMEM_BIG = 56 * 1024 * 1024
VMEM_MID = 48 * 1024 * 1024


def _params(sem, vmem=VMEM_MID):
    return pltpu.CompilerParams(dimension_semantics=sem, vmem_limit_bytes=vmem)


def _sigmoid(x):
    return 1.0 / (1.0 + jnp.exp(-x))


def _norm_kernel(x_ref, g_ref, sc_ref, sh_ref, o_ref):
    x = x_ref[...]
    ms = jnp.mean(x * x, axis=-1, keepdims=True)
    y = x * lax.rsqrt(ms + RMS_EPS) * g_ref[...]
    o_ref[...] = (y * (1.0 + sc_ref[...]) + sh_ref[...]).astype(o_ref.dtype)


def _final_norm_kernel(x_ref, g_ref, o_ref):
    x = x_ref[...]
    ms = jnp.mean(x * x, axis=-1, keepdims=True)
    o_ref[...] = (x * lax.rsqrt(ms + RMS_EPS) * g_ref[...]).astype(o_ref.dtype)


def _norm_mod(x, g, sc, sh, *, bm, rows_per_mod):
    m, d = x.shape
    r = sc.shape[1]
    mod_spec = pl.BlockSpec((None, r, d), lambda i: ((i * bm) // rows_per_mod, 0, 0))
    return pl.pallas_call(
        _norm_kernel,
        grid=(m // bm,),
        in_specs=[pl.BlockSpec((bm, d), lambda i: (i, 0)),
                  pl.BlockSpec((1, d), lambda i: (0, 0)),
                  mod_spec, mod_spec],
        out_specs=pl.BlockSpec((bm, d), lambda i: (i, 0)),
        out_shape=jax.ShapeDtypeStruct((m, d), BF16),
        compiler_params=_params(("arbitrary",)),
        name="norm_mod",
    )(x, g, sc, sh)


def _final_norm(x, g, *, bm):
    m, d = x.shape
    return pl.pallas_call(
        _final_norm_kernel,
        grid=(m // bm,),
        in_specs=[pl.BlockSpec((bm, d), lambda i: (i, 0)),
                  pl.BlockSpec((1, d), lambda i: (0, 0))],
        out_specs=pl.BlockSpec((bm, d), lambda i: (i, 0)),
        out_shape=jax.ShapeDtypeStruct((m, d), F32),
        compiler_params=_params(("arbitrary",)),
        name="final_norm",
    )(x, g)


def _cast_weight(w_ref, wb_ref):
    k = w_ref.shape[0]
    chunk = 512 if k % 512 == 0 else k

    def body(c, _):
        r0 = pl.multiple_of(c * chunk, chunk)
        wb_ref[pl.ds(r0, chunk), :] = w_ref[pl.ds(r0, chunk), :].astype(BF16)
        return 0

    lax.fori_loop(0, k // chunk, body, 0)


def _rope_tile(acc, tab_ref):
    cos = tab_ref[:, 0:HEAD_DIM]
    s_lo = tab_ref[:, HEAD_DIM:2 * HEAD_DIM]
    s_hi = tab_ref[:, 2 * HEAD_DIM:3 * HEAD_DIM]
    half = ROT_DIM // 2
    pieces = []
    for hh in range(acc.shape[1] // HEAD_DIM):
        x = acc[:, hh * HEAD_DIM:(hh + 1) * HEAD_DIM]
        nxt = pltpu.roll(x, HEAD_DIM - half, axis=1)
        prv = pltpu.roll(x, half, axis=1)
        pieces.append(x * cos + nxt * s_lo + prv * s_hi)
    return pieces


def _mm_kernel(*refs, mode, silu_in, qkv_width, out_scale):
    if mode == "plain":
        x_ref, w_ref, o_ref, wb_ref = refs
    elif mode == "bias":
        x_ref, w_ref, b_ref, o_ref, wb_ref = refs
    elif mode == "resid":
        x_ref, w_ref, res_ref, gate_ref, o_ref, wb_ref = refs
    elif mode == "rope":
        x_ref, w_ref, tab_ref, o_ref, wb_ref = refs
    else:
        raise ValueError(mode)

    @pl.when(pl.program_id(1) == 0)
    def _():
        _cast_weight(w_ref, wb_ref)

    x = x_ref[...]
    if silu_in:
        x = x * _sigmoid(x)
    acc = jnp.dot(x.astype(BF16), wb_ref[...], preferred_element_type=F32)
    if mode == "plain":
        if out_scale != 1.0:
            acc = acc * out_scale
        o_ref[...] = acc.astype(o_ref.dtype)
    elif mode == "bias":
        o_ref[...] = (acc + b_ref[...]).astype(o_ref.dtype)
    elif mode == "resid":
        o_ref[...] = (res_ref[...] + gate_ref[...] * acc).astype(o_ref.dtype)
    else:
        bn = acc.shape[1]
        kind = ((pl.program_id(0) * bn) // qkv_width) % 3

        @pl.when(kind == 2)
        def _():
            o_ref[...] = acc.astype(o_ref.dtype)

        @pl.when(kind != 2)
        def _():
            for hh, piece in enumerate(_rope_tile(acc, tab_ref)):
                o_ref[:, hh * HEAD_DIM:(hh + 1) * HEAD_DIM] = piece.astype(o_ref.dtype)


def _matmul(x, w, layer, *, bm, bn, out_dtype, mode="plain", extras=(), extra_specs=(),
            silu_in=False, qkv_width=0, out_scale=1.0, vmem=VMEM_BIG, name="matmul"):
    m, k = x.shape
    n = w.shape[2]
    kern = functools.partial(_mm_kernel, mode=mode, silu_in=silu_in, qkv_width=qkv_width, out_scale=out_scale)
    return pl.pallas_call(
        kern,
        grid=(n // bn, m // bm),
        in_specs=[pl.BlockSpec((bm, k), lambda j, i: (i, 0)),
                  pl.BlockSpec((None, k, bn), lambda j, i: (layer, 0, j)),
                  *extra_specs],
        out_specs=pl.BlockSpec((bm, bn), lambda j, i: (i, j)),
        out_shape=jax.ShapeDtypeStruct((m, n), out_dtype),
        scratch_shapes=[pltpu.VMEM((k, bn), BF16)],
        compiler_params=_params(("arbitrary", "arbitrary"), vmem),
        name=name,
    )(x, w, *extras)


def _mm_ktiled_kernel(*refs, nk_big, nk_small):
    if nk_small:
        xb_ref, wb_ref, xs_ref, ws_ref, res_ref, gate_ref, o_ref, acc_ref = refs
    else:
        xb_ref, wb_ref, res_ref, gate_ref, o_ref, acc_ref = refs
    kk = pl.program_id(2)

    @pl.when(kk == 0)
    def _():
        acc_ref[...] = jnp.zeros_like(acc_ref)

    @pl.when(kk < nk_big)
    def _():
        acc_ref[...] += jnp.dot(xb_ref[...], wb_ref[...].astype(BF16), preferred_element_type=F32)

    if nk_small:
        @pl.when(kk >= nk_big)
        def _():
            acc_ref[...] += jnp.dot(xs_ref[...], ws_ref[...].astype(BF16), preferred_element_type=F32)

    @pl.when(kk == nk_big + nk_small - 1)
    def _():
        o_ref[...] = res_ref[...] + gate_ref[...] * acc_ref[...]


def _matmul_ktiled_resid(x, w, layer, res, gate, *, bm, bn, rows_per_mod, name="down"):
    m, k = x.shape
    n = w.shape[2]
    r = gate.shape[1]
    bk_big = min(k, K_BLOCK_BIG)
    nk_big = k // bk_big
    rem = k - nk_big * bk_big
    bk_small = _pick(rem, (256, 128)) if rem else 0
    nk_small = rem // bk_small if rem else 0
    small0 = (nk_big * bk_big) // bk_small if rem else 0
    assert rem == 0 or (nk_big * bk_big) % bk_small == 0

    def big(kk):
        return jnp.minimum(kk, nk_big - 1)

    def small(kk):
        return small0 + jnp.maximum(kk - nk_big, 0)

    in_specs = [pl.BlockSpec((bm, bk_big), lambda j, i, kk: (i, big(kk))),
                pl.BlockSpec((None, bk_big, bn), lambda j, i, kk: (layer, big(kk), j))]
    args = [x, w]
    if nk_small:
        in_specs += [pl.BlockSpec((bm, bk_small), lambda j, i, kk: (i, small(kk))),
                     pl.BlockSpec((None, bk_small, bn), lambda j, i, kk: (layer, small(kk), j))]
        args += [x, w]
    in_specs += [pl.BlockSpec((bm, bn), lambda j, i, kk: (i, j)),
                 pl.BlockSpec((None, r, bn), lambda j, i, kk: ((i * bm) // rows_per_mod, 0, j))]
    return pl.pallas_call(
        functools.partial(_mm_ktiled_kernel, nk_big=nk_big, nk_small=nk_small),
        grid=(n // bn, m // bm, nk_big + nk_small),
        in_specs=in_specs,
        out_specs=pl.BlockSpec((bm, bn), lambda j, i, kk: (i, j)),
        out_shape=jax.ShapeDtypeStruct((m, n), F32),
        scratch_shapes=[pltpu.VMEM((bm, bn), F32)],
        compiler_params=_params(("arbitrary", "arbitrary", "arbitrary"), VMEM_MID),
        name=name,
    )(*args, res, gate)


def _ffn_up_kernel(*refs, bm, sub, halo, tap, blocks_per_seq, has_prev):
    if has_prev:
        (h_ref, wg_ref, wu_ref, cwg_ref, cwu_ref, cbg_ref, cbu_ref, pg_ref, pu_ref,
         act_ref, stg_ref, stu_ref, wgb, wub, ug, uu) = refs
    else:
        (h_ref, wg_ref, wu_ref, cwg_ref, cwu_ref, cbg_ref, cbu_ref,
         act_ref, stg_ref, stu_ref, wgb, wub, ug, uu) = refs
    i = pl.program_id(1)

    @pl.when(i == 0)
    def _():
        _cast_weight(wg_ref, wgb)
        _cast_weight(wu_ref, wub)

    @pl.when(i % blocks_per_seq == 0)
    def _():
        if has_prev:
            ug[0:halo, :] = pg_ref[...]
            uu[0:halo, :] = pu_ref[...]
        else:
            ug[0:halo, :] = jnp.zeros((halo, ug.shape[1]), F32)
            uu[0:halo, :] = jnp.zeros((halo, uu.shape[1]), F32)

    def conv(u, cw_ref, cb_ref, r0, rows):
        y = cb_ref[...] + cw_ref[0:1, :] * u[halo - 2 * tap + r0:halo - 2 * tap + r0 + rows, :]
        y = y + cw_ref[1:2, :] * u[halo - tap + r0:halo - tap + r0 + rows, :]
        return y + cw_ref[2:3, :] * u[halo + r0:halo + r0 + rows, :]

    for r0 in range(0, bm, sub):
        h = h_ref[r0:r0 + sub, :]
        ug[halo + r0:halo + r0 + sub, :] = jnp.dot(h, wgb[...], preferred_element_type=F32)
        uu[halo + r0:halo + r0 + sub, :] = jnp.dot(h, wub[...], preferred_element_type=F32)
        gate = conv(ug, cwg_ref, cbg_ref, r0, sub)
        up = conv(uu, cwu_ref, cbu_ref, r0, sub)
        act_ref[r0:r0 + sub, :] = (gate * _sigmoid(gate) * up).astype(act_ref.dtype)
    tail_g = ug[bm:bm + halo, :]
    tail_u = uu[bm:bm + halo, :]
    stg_ref[...] = tail_g
    stu_ref[...] = tail_u
    ug[0:halo, :] = tail_g
    uu[0:halo, :] = tail_u


def _ffn_up(h, w_up, w_conv, b_conv3, layer, *, bm, bn, tap, rows_per_seq, prev=None):
    m, d = h.shape
    d_ff = w_up.shape[2] // 2
    nj = d_ff // bn
    halo = max(SUBLANES, 2 * tap)
    has_prev = prev is not None
    kern = functools.partial(_ffn_up_kernel, bm=bm, sub=_pick(bm, (FFN_SUB,)), halo=halo, tap=tap,
                             blocks_per_seq=rows_per_seq // bm, has_prev=has_prev)
    in_specs = [pl.BlockSpec((bm, d), lambda j, i: (i, 0)),
                pl.BlockSpec((None, d, bn), lambda j, i: (layer, 0, j)),
                pl.BlockSpec((None, d, bn), lambda j, i: (layer, 0, j + nj)),
                pl.BlockSpec((None, CONV_W, bn), lambda j, i: (layer, 0, j)),
                pl.BlockSpec((None, CONV_W, bn), lambda j, i: (layer, 0, j + nj)),
                pl.BlockSpec((None, 1, bn), lambda j, i: (layer, 0, j)),
                pl.BlockSpec((None, 1, bn), lambda j, i: (layer, 0, j + nj))]
    args = [h, w_up, w_up, w_conv, w_conv, b_conv3, b_conv3]
    if has_prev:
        in_specs += [pl.BlockSpec((halo, bn), lambda j, i: (0, j)),
                     pl.BlockSpec((halo, bn), lambda j, i: (0, j + nj))]
        args += [prev, prev]
    st_spec = pl.BlockSpec((None, halo, bn), lambda j, i: (i, 0, j))
    return pl.pallas_call(
        kern,
        grid=(nj, m // bm),
        in_specs=in_specs,
        out_specs=[pl.BlockSpec((bm, bn), lambda j, i: (i, j)), st_spec, st_spec],
        out_shape=[jax.ShapeDtypeStruct((m, d_ff), BF16),
                   jax.ShapeDtypeStruct((m // bm, halo, d_ff), F32),
                   jax.ShapeDtypeStruct((m // bm, halo, d_ff), F32)],
        scratch_shapes=[pltpu.VMEM((d, bn), BF16), pltpu.VMEM((d, bn), BF16),
                        pltpu.VMEM((bm + halo, bn), F32), pltpu.VMEM((bm + halo, bn), F32)],
        compiler_params=_params(("arbitrary", "arbitrary"), VMEM_BIG),
        name="ffn_up",
    )(*args)


def _attn_a_prompt_kernel(q0, k0, v0, q1, k1, v1, q2, k2, v2, o_ref, og, lg, *, seq):
    qkv = ((q0, k0, v0), (q1, k1, v1), (q2, k2, v2))
    scale = HEAD_DIM ** -0.5
    row = lax.broadcasted_iota(jnp.int32, (BAND, 2 * BAND), 0)
    col = lax.broadcasted_iota(jnp.int32, (BAND, 2 * BAND), 1)
    cur_ok = jnp.logical_and(col >= BAND, col - BAND <= row)
    prev_ok = jnp.logical_and(col < BAND, col >= row)
    dn = (((1,), (1,)), ((), ()))

    for g in range(N_GROUPS_A):
        dil = DILATIONS[g]
        nb = seq // dil // BAND
        q_ref, k_ref, v_ref = qkv[g]
        unroll = UNROLL_A if (dil * nb) % UNROLL_A == 0 else 1

        def rows(start, dil=dil):
            if dil == 1:
                return pl.ds(pl.multiple_of(start, BAND), BAND)
            return pl.ds(start, BAND, stride=dil)

        def body(it, _, g=g, dil=dil, nb=nb, q_ref=q_ref, k_ref=k_ref, v_ref=v_ref, rows=rows, unroll=unroll):
            starts, oks, scores, vbands = [], [], [], []
            for u in range(unroll):
                idx = it * unroll + u
                blk = idx % nb
                start = idx // nb + blk * (BAND * dil)
                has_prev = blk > 0
                pstart = jnp.where(has_prev, start - BAND * dil, start)
                q = q_ref[rows(start), :].astype(BF16)
                kband = jnp.concatenate([k_ref[rows(pstart), :], k_ref[rows(start), :]], axis=0).astype(BF16)
                vbands.append(jnp.concatenate([v_ref[rows(pstart), :], v_ref[rows(start), :]], axis=0).astype(BF16))
                scores.append(lax.dot_general(q, kband, dn, preferred_element_type=F32))
                starts.append(start)
                oks.append(jnp.logical_or(cur_ok, jnp.logical_and(prev_ok, has_prev)))
            probs, dens, lses = [], [], []
            for u in range(unroll):
                sc = jnp.where(oks[u], scores[u] * scale, NEG_BIG)
                mx = jnp.max(sc, axis=1, keepdims=True)
                p = jnp.exp(sc - mx)
                den = jnp.sum(p, axis=1, keepdims=True)
                probs.append(p.astype(BF16))
                dens.append(den)
                lses.append(mx + jnp.log(den))
            for u in range(unroll):
                o = jnp.dot(probs[u], vbands[u], preferred_element_type=F32)
                og[g, rows(starts[u]), :] = o / dens[u]
                lg[g, rows(starts[u]), :] = jnp.broadcast_to(lses[u], (BAND, HEAD_DIM))
            return 0

        lax.fori_loop(0, dil * nb // unroll, body, 0)

    chunk = 256

    def comb(c, _):
        r0 = pl.multiple_of(c * chunk, chunk)
        l0 = lg[0, pl.ds(r0, chunk), :]
        l1 = lg[1, pl.ds(r0, chunk), :]
        l2 = lg[2, pl.ds(r0, chunk), :]
        mx = jnp.maximum(jnp.maximum(l0, l1), l2)
        e0 = jnp.exp(l0 - mx)
        e1 = jnp.exp(l1 - mx)
        e2 = jnp.exp(l2 - mx)
        tot = e0 + e1 + e2
        o = (og[0, pl.ds(r0, chunk), :] * (e0 / tot) + og[1, pl.ds(r0, chunk), :] * (e1 / tot)
             + og[2, pl.ds(r0, chunk), :] * (e2 / tot))
        o_ref[pl.ds(r0, chunk), :] = o.astype(o_ref.dtype)
        return 0

    lax.fori_loop(0, seq // chunk, comb, 0)


def _attn_a_prompt(qkv, *, batch, seq, heads):
    m = qkv.shape[0]
    specs = []
    for g in range(N_GROUPS_A):
        for t in range(3):
            specs.append(pl.BlockSpec((seq, HEAD_DIM),
                                      lambda n, h, g=g, t=t: (n, (g * 3 + t) * heads + h)))
    return pl.pallas_call(
        functools.partial(_attn_a_prompt_kernel, seq=seq),
        grid=(batch, heads),
        in_specs=specs,
        out_specs=pl.BlockSpec((seq, HEAD_DIM), lambda n, h: (n, h)),
        out_shape=jax.ShapeDtypeStruct((m, heads * HEAD_DIM), BF16),
        scratch_shapes=[pltpu.VMEM((N_GROUPS_A, seq, HEAD_DIM), F32),
                        pltpu.VMEM((N_GROUPS_A, seq, HEAD_DIM), F32)],
        compiler_params=_params(("arbitrary", "arbitrary")),
        name="attn_a_prompt",
    )(*([qkv] * 9))


def _attn_a_sample_kernel(qn_ref, c0_ref, c1_ref, c2_ref, o_ref, *, heads, dec_seq):
    width = heads * HEAD_DIM
    scale = HEAD_DIM ** -0.5
    caches = (c0_ref, c1_ref, c2_ref)
    hrow = lax.broadcasted_iota(jnp.int32, (heads, width), 0)
    hcol = lax.broadcasted_iota(jnp.int32, (heads, width), 1) // HEAD_DIM
    head_mask = hrow == hcol
    key_idx = lax.broadcasted_iota(jnp.int32, (heads, BAND), 1)
    dn = (((1,), (1,)), ((), ()))

    def split_heads(x):
        y = pltpu.einshape("mrd->rmd", x)
        k = jnp.concatenate([y[h] for h in range(heads)], axis=1).astype(BF16)
        v = jnp.concatenate([y[heads + h] for h in range(heads)], axis=1).astype(BF16)
        return k, v

    rh = 2 * heads
    kv_bufs = [[split_heads(caches[0][...])]]
    for g in range(1, N_GROUPS_A):
        kv_bufs.append([split_heads(caches[g][:, t * rh:(t + 1) * rh, :]) for t in range(dec_seq)])

    for t in range(dec_seq):
        outs, lses = [], []
        for g in range(N_GROUPS_A):
            base = g * 3 * width
            q_row = qn_ref[t:t + 1, base:base + width]
            qbd = jnp.where(head_mask, jnp.broadcast_to(q_row, (heads, width)), 0.0)
            kb, vb = kv_bufs[g][0 if g == 0 else t]
            s_buf = lax.dot_general(qbd.astype(BF16), kb, dn, preferred_element_type=F32) * scale
            if g == 0:
                s_buf = jnp.where(key_idx >= t, s_buf, NEG_BIG)
                new_rows = range(t + 1)
            else:
                new_rows = (t,)
            s_new = []
            for tn in new_rows:
                k_row = qn_ref[tn:tn + 1, base + width:base + 2 * width]
                s_new.append(jnp.sum(qbd * k_row, axis=1, keepdims=True) * scale)
            mx = jnp.max(s_buf, axis=1, keepdims=True)
            for s in s_new:
                mx = jnp.maximum(mx, s)
            p_buf = jnp.exp(s_buf - mx)
            den = jnp.sum(p_buf, axis=1, keepdims=True)
            o = jnp.dot(p_buf.astype(BF16), vb, preferred_element_type=F32)
            for tn, s in zip(new_rows, s_new):
                p = jnp.exp(s - mx)
                den = den + p
                o = o + p * qn_ref[tn:tn + 1, base + 2 * width:base + 3 * width]
            o = jnp.where(head_mask, o / den, 0.0)
            lse = jnp.where(head_mask, jnp.broadcast_to(mx + jnp.log(den), (heads, width)), 0.0)
            outs.append(jnp.sum(o, axis=0, keepdims=True))
            lses.append(jnp.sum(lse, axis=0, keepdims=True))
        mx = jnp.maximum(jnp.maximum(lses[0], lses[1]), lses[2])
        es = [jnp.exp(l - mx) for l in lses]
        tot = es[0] + es[1] + es[2]
        row = outs[0] * (es[0] / tot) + outs[1] * (es[1] / tot) + outs[2] * (es[2] / tot)
        o_ref[t:t + 1, :] = row.astype(o_ref.dtype)


def _attn_a_sample(qn, caches, layer, *, heads):
    db, dec_seq, _ = qn.shape
    width = heads * HEAD_DIM
    views, specs = [], []
    rh = 2 * heads
    for g in range(N_GROUPS_A):
        c = caches[g]
        nlay, _, nbuf = c.shape[:3]
        dil = DILATIONS[g]
        views.append(c.reshape(nlay, db, nbuf // dil, dil * rh, HEAD_DIM))
        need = rh if g == 0 else dec_seq * rh
        blk_r = need if need == dil * rh else -(-need // SUBLANES) * SUBLANES
        specs.append(pl.BlockSpec((None, None, nbuf // dil, blk_r, HEAD_DIM), lambda n: (layer, n, 0, 0, 0)))
    return pl.pallas_call(
        functools.partial(_attn_a_sample_kernel, heads=heads, dec_seq=dec_seq),
        grid=(db,),
        in_specs=[pl.BlockSpec((None, dec_seq, qn.shape[2]), lambda n: (n, 0, 0)), *specs],
        out_specs=pl.BlockSpec((None, dec_seq, width), lambda n: (n, 0, 0)),
        out_shape=jax.ShapeDtypeStruct((db, dec_seq, width), F32),
        compiler_params=_params(("arbitrary",)),
        name="attn_a_sample",
    )(qn, *views)


LOG2E = 1.4426950408889634
LN2 = 0.6931471805599453
SB_QSCALE = HEAD_DIM ** -0.5 * LOG2E


def _sb_blocks(x2s, tri, masks, vs):
    sps = []
    for x2, mask in zip(x2s, masks):
        neg_abs = pltpu.bitcast(pltpu.bitcast(x2, jnp.uint32) | jnp.uint32(0x80000000), F32)
        sp = LN2 * jnp.maximum(x2, 0.0) + jnp.log(1.0 + jnp.exp2(neg_abs))
        sps.append(sp if mask is None else jnp.where(mask, sp, 0.0))
    incls = [jnp.dot(sp.astype(BF16), tri, preferred_element_type=F32) for sp in sps]
    ws = []
    for x2, incl, mask in zip(x2s, incls, masks):
        a = jnp.exp2(x2 - LOG2E * incl)
        ws.append((a if mask is None else jnp.where(mask, a, 0.0)).astype(BF16))
    pvs = [jnp.dot(a, v, preferred_element_type=F32) for a, v in zip(ws, vs)]
    return [(pv, jnp.sum(sp, axis=1, keepdims=True)) for pv, sp in zip(pvs, sps)]


def _sb_prompt_kernel(q_ref, k_ref, v_ref, b_ref, tri_ref, o_ref, kb_ref, vb_ref, *, seq, blk, hpb):
    nblk = seq // blk
    chunk = 512 if seq % 512 == 0 else seq

    def cast(c, _):
        r0 = pl.multiple_of(c * chunk, chunk)
        kb_ref[pl.ds(r0, chunk), :] = k_ref[pl.ds(r0, chunk), :].astype(BF16)
        vb_ref[pl.ds(r0, chunk), :] = v_ref[pl.ds(r0, chunk), :].astype(BF16)
        return 0

    lax.fori_loop(0, seq // chunk, cast, 0)
    row = lax.broadcasted_iota(jnp.int32, (blk, blk), 0)
    col = lax.broadcasted_iota(jnp.int32, (blk, blk), 1)
    diag_mask = col < row
    dn = (((1,), (1,)), ((), ()))
    lanes = [slice(hh * HEAD_DIM, (hh + 1) * HEAD_DIM) for hh in range(hpb)]

    def q_block(qb, _):
        q0 = pl.multiple_of(qb * blk, blk)
        qs = [q_ref[pl.ds(q0, blk), ln] for ln in lanes]

        def blocks(kbs, mask):
            k0s = [pl.multiple_of(kb * blk, blk) for kb in kbs]
            x2s = [lax.dot_general(qs[hh], kb_ref[pl.ds(k0, blk), lanes[hh]], dn,
                                   preferred_element_type=F32) + b_ref[hh]
                   for hh in range(hpb) for k0 in k0s]
            vs = [vb_ref[pl.ds(k0, blk), lanes[hh]] for hh in range(hpb) for k0 in k0s]
            out = _sb_blocks(x2s, tri_ref[...], [mask] * len(x2s), vs)
            return [out[hh * len(kbs):(hh + 1) * len(kbs)] for hh in range(hpb)]

        state = tuple(res[0] for res in blocks([qb], diag_mask))

        def pair(it, state):
            res = blocks([qb - 1 - 2 * it, qb - 2 - 2 * it], None)
            out = []
            for (acc, carry), ((pv0, m0), (pv1, m1)) in zip(state, res):
                acc = acc + jnp.exp(-carry) * pv0 + jnp.exp(-(carry + m0)) * pv1
                out.append((acc, carry + m0 + m1))
            return tuple(out)

        state = lax.fori_loop(0, qb // 2, pair, state)

        def last(state):
            res = blocks([0], None)
            return tuple((acc + jnp.exp(-carry) * pv, carry + m)
                         for (acc, carry), ((pv, m),) in zip(state, res))

        state = lax.cond(qb % 2 == 1, last, lambda st: st, state)
        for hh, (acc, _) in enumerate(state):
            o_ref[pl.ds(q0, blk), lanes[hh]] = acc.astype(o_ref.dtype)
        return 0

    lax.fori_loop(0, nblk, q_block, 0)


def _tri(blk):
    r = lax.broadcasted_iota(jnp.int32, (blk, blk), 0)
    c = lax.broadcasted_iota(jnp.int32, (blk, blk), 1)
    return (r >= c).astype(BF16)


def _sb_prompt(q, kv, bias, *, batch, seq, heads, blk, hpb=4):
    m = q.shape[0]
    hpb = hpb if heads % hpb == 0 else 1
    w = hpb * HEAD_DIM
    bias_b = jnp.broadcast_to(LOG2E * bias.astype(F32)[:, None, None], (heads, 1, blk))
    return pl.pallas_call(
        functools.partial(_sb_prompt_kernel, seq=seq, blk=blk, hpb=hpb),
        grid=(batch, heads // hpb),
        in_specs=[pl.BlockSpec((seq, w), lambda n, h: (n, h)),
                  pl.BlockSpec((seq, w), lambda n, h: (n, h)),
                  pl.BlockSpec((seq, w), lambda n, h: (n, heads // hpb + h)),
                  pl.BlockSpec((hpb, 1, blk), lambda n, h: (h, 0, 0)),
                  pl.BlockSpec((blk, blk), lambda n, h: (0, 0))],
        out_specs=pl.BlockSpec((seq, w), lambda n, h: (n, h)),
        out_shape=jax.ShapeDtypeStruct((m, heads * HEAD_DIM), BF16),
        scratch_shapes=[pltpu.VMEM((seq, w), BF16), pltpu.VMEM((seq, w), BF16)],
        compiler_params=_params(("arbitrary", "arbitrary")),
        name="sb_prompt",
    )(q, kv, kv, bias_b, _tri(blk))


def _sb_sample_kernel(pt_ref, q_ref, new_ref, page_ref, b_ref, tri_ref, o_ref,
                      qbd_ref, acc_ref, carry_ref, *, heads, dec_seq, page):
    del pt_ref
    width = heads * HEAD_DIM
    rows = dec_seq * heads
    p = pl.program_id(1)
    hrow = lax.broadcasted_iota(jnp.int32, (heads, width), 0)
    hcol = lax.broadcasted_iota(jnp.int32, (heads, width), 1) // HEAD_DIM
    head_mask = hrow == hcol
    dn = (((1,), (1,)), ((), ()))

    def gather(kv_ref, first):
        x = kv_ref[...].reshape(page, 2 * heads, HEAD_DIM)[:, first:first + heads, :]
        y = pltpu.einshape("khd->hkd", x)
        return jnp.concatenate([y[h] for h in range(heads)], axis=1).astype(BF16)

    def process(kv_ref, mask):
        x2 = lax.dot_general(qbd_ref[...], gather(kv_ref, 0), dn, preferred_element_type=F32) + b_ref[...]
        ((pv, mass),) = _sb_blocks([x2], tri_ref[...], [mask], [gather(kv_ref, heads)])
        carry = carry_ref[:, 0:1]
        acc_ref[...] += jnp.exp(-carry) * pv
        carry_ref[...] = jnp.broadcast_to(carry + mass, carry_ref.shape)

    @pl.when(p == 0)
    def _():
        for t in range(dec_seq):
            q_row = jnp.broadcast_to(q_ref[t:t + 1, :], (heads, width))
            qbd_ref[t * heads:(t + 1) * heads, :] = jnp.where(head_mask, q_row, 0.0).astype(BF16)
        acc_ref[...] = jnp.zeros_like(acc_ref)
        carry_ref[...] = jnp.zeros_like(carry_ref)
        t_of_row = lax.broadcasted_iota(jnp.int32, (rows, page), 0) // heads
        key = lax.broadcasted_iota(jnp.int32, (rows, page), 1)
        process(new_ref, key < t_of_row)

    @pl.when(p > 0)
    def _():
        process(page_ref, None)

    @pl.when(p == pl.num_programs(1) - 1)
    def _():
        for t in range(dec_seq):
            o = jnp.where(head_mask, acc_ref[t * heads:(t + 1) * heads, :], 0.0)
            o_ref[t:t + 1, :] = jnp.sum(o, axis=0, keepdims=True).astype(o_ref.dtype)


def _sb_sample(q, kv_new_pad, pages, page_table, bias, *, heads, page):
    db, dec_seq, width = q.shape
    n_pages = page_table.shape[1]
    rows = dec_seq * heads
    prow = page * 2 * heads
    bias_b = jnp.broadcast_to(LOG2E * jnp.tile(bias.astype(F32), dec_seq)[:, None], (rows, page))

    def page_map(n, p, pt):
        return (pt[n, n_pages - jnp.maximum(p, 1)], 0, 0)

    grid_spec = pltpu.PrefetchScalarGridSpec(
        num_scalar_prefetch=1,
        grid=(db, n_pages + 1),
        in_specs=[pl.BlockSpec((None, dec_seq, width), lambda n, p, pt: (n, 0, 0)),
                  pl.BlockSpec((None, prow, HEAD_DIM), lambda n, p, pt: (n, 0, 0)),
                  pl.BlockSpec((None, prow, HEAD_DIM), page_map),
                  pl.BlockSpec((rows, page), lambda n, p, pt: (0, 0)),
                  pl.BlockSpec((page, page), lambda n, p, pt: (0, 0))],
        out_specs=pl.BlockSpec((None, dec_seq, width), lambda n, p, pt: (n, 0, 0)),
        scratch_shapes=[pltpu.VMEM((rows, width), BF16), pltpu.VMEM((rows, width), F32),
                        pltpu.VMEM((rows, HEAD_DIM), F32)],
    )
    return pl.pallas_call(
        functools.partial(_sb_sample_kernel, heads=heads, dec_seq=dec_seq, page=page),
        grid_spec=grid_spec,
        out_shape=jax.ShapeDtypeStruct((db, dec_seq, width), F32),
        compiler_params=_params(("arbitrary", "arbitrary")),
        name="sb_sample",
    )(page_table, q, kv_new_pad, pages, bias_b, _tri(page))


def _rope_table(pos):
    half = ROT_DIM // 2
    inv = ROPE_THETA ** (-jnp.arange(half, dtype=F32) / half)
    ang = pos.astype(F32)[:, None] * inv[None, :]
    cos, sin = jnp.cos(ang), jnp.sin(ang)
    t = pos.shape[0]
    ones = jnp.ones((t, HEAD_DIM - ROT_DIM), F32)
    zeros = jnp.zeros((t, HEAD_DIM - ROT_DIM), F32)
    zh = jnp.zeros((t, half), F32)
    return jnp.concatenate([cos, cos, ones, -sin, zh, zeros, zh, sin, zeros], axis=1)


def _pick(v, cands):
    for c in cands:
        if v % c == 0:
            return c
    return v


def _trunk(x, mods, mod_kv, tabs, cfg, weights, sample):
    (norm_attn_g, norm_ffn_g, w_qkv_a, w_o_a, w_q_b, w_o_b, sb_bias, norm_kv_g, w_kv,
     w_up, w_conv, b_conv3, w_down, norm_final_g) = weights
    m, d = x.shape
    bm, bmn, rpm = cfg["bm"], cfg["bm_norm"], cfg["rows_per_mod"]
    depth = w_up.shape[0]
    n_a = w_qkv_a.shape[0]
    heads_a = w_o_a.shape[1] // HEAD_DIM
    heads_b = w_q_b.shape[2] // HEAD_DIM
    d_ff = w_down.shape[1]
    r = mods[0][0].shape[1]

    def mod_spec(bn):
        return pl.BlockSpec((None, r, bn), lambda j, i: ((i * bm) // rpm, 0, j))

    def tile(n, cands=(512, 256, 128)):
        return _pick(n, cands)

    qkv_all, tails = [], []
    kv = None
    for l in range(depth):
        sh1, sc1, g1, sh2, sc2, g2 = mods[l]
        h = _norm_mod(x, norm_attn_g[l][None], sc1, sh1, bm=bmn, rows_per_mod=rpm)
        if l < n_a:
            bn = tile(heads_a * HEAD_DIM)
            tab_spec = pl.BlockSpec((bm, 3 * HEAD_DIM), cfg["tab_map"])
            qkv = _matmul(h, w_qkv_a, l, bm=bm, bn=bn, out_dtype=F32, mode="rope",
                          extras=(tabs,), extra_specs=(tab_spec,),
                          qkv_width=heads_a * HEAD_DIM, name="qkv")
            qkv_all.append(qkv)
            o = cfg["attn_a"](qkv, l)
            w_o, lo = w_o_a, l
        else:
            j = l - n_a
            q = _matmul(h, w_q_b, j, bm=bm, bn=tile(w_q_b.shape[2]), out_dtype=cfg["q_dtype"],
                        out_scale=SB_QSCALE, name="q_b")
            o = cfg["attn_b"](q, kv, sb_bias[j])
            w_o, lo = w_o_b, j
        bn = tile(d)
        x = _matmul(o, w_o, lo, bm=bm, bn=bn, out_dtype=F32, mode="resid",
                    extras=(x, g1), extra_specs=(pl.BlockSpec((bm, bn), lambda j, i: (i, j)), mod_spec(bn)),
                    name="attn_out")
        h = _norm_mod(x, norm_ffn_g[l][None], sc2, sh2, bm=bmn, rows_per_mod=rpm)
        act, tail_g, tail_u = _ffn_up(h, w_up, w_conv, b_conv3, l, bm=bm, bn=_pick(d_ff, (256, 128)),
                                      tap=cfg["tap"], rows_per_seq=cfg["rows_per_seq"],
                                      prev=sample["prev"][l] if sample else None)
        tails.append((tail_g, tail_u))
        x = _matmul_ktiled_resid(act, w_down, l, x, g2, bm=cfg["bm_down"], bn=_pick(d, (cfg["bn_down"], 256, 128)),
                                 rows_per_mod=rpm)
        if l == n_a - 1:
            sh_kv, sc_kv = mod_kv
            h_kv = _norm_mod(x, norm_kv_g[None], sc_kv, sh_kv, bm=bmn, rows_per_mod=rpm)
            kv = _matmul(h_kv, w_kv[None], 0, bm=bm, bn=tile(w_kv.shape[1]), out_dtype=F32, name="kv")
    y = _final_norm(x, norm_final_g[None], bm=bmn)
    return y, qkv_all, kv, tails


def kernel(x_prompt, x_sample, cache_win0, cache_win1, cache_win2, cache_kv_pages, state_conv, page_table,
           c_prompt, c_sample, norm_attn_g, norm_ffn_g, w_ada, b_ada, w_qkv_a, w_o_a, w_q_b, w_o_b, sb_bias,
           norm_kv_g, w_ada_kv, b_ada_kv, w_kv, w_up, w_conv, b_conv, w_down, norm_final_g):
    batch, seq, d = x_prompt.shape
    db, dec_seq, _ = x_sample.shape
    depth = w_up.shape[0]
    n_a = w_qkv_a.shape[0]
    heads_a = w_o_a.shape[1] // HEAD_DIM
    heads_b = w_q_b.shape[2] // HEAD_DIM
    width_a = heads_a * HEAD_DIM
    width_b = heads_b * HEAD_DIM
    d_ff = w_down.shape[1]
    page = cache_kv_pages.shape[1]
    past_len = page_table.shape[1] * page
    caches = (cache_win0, cache_win1, cache_win2)
    assert seq % (BAND * DILATIONS[-1]) == 0 and batch + db <= ADA_ROWS and db == SUBLANES
    assert all(caches[g].shape[2] == WINDOWS[g] for g in range(N_GROUPS_A)) and dec_seq <= DILATIONS[1]

    c_all = jnp.concatenate([c_prompt, c_sample, jnp.zeros((ADA_ROWS - batch - db, d), F32)], axis=0)
    def ada(w, b, layer):
        n = w.shape[2]
        bn = _pick(n, (1024, 512, 256, 128))
        return _matmul(c_all, w, layer, bm=ADA_ROWS, bn=bn, out_dtype=F32, mode="bias", silu_in=True,
                       extras=(b.reshape(w.shape[0], 1, n),),
                       extra_specs=(pl.BlockSpec((None, 1, bn), lambda j, i: (layer, 0, j)),), name="ada")

    mod_all = [ada(w_ada, b_ada, l) for l in range(depth)]
    mod_kv_all = ada(w_ada_kv[None], b_ada_kv[None], 0)

    def prompt_mods(a, chunks):
        return tuple(c.reshape(batch, 1, d) for c in jnp.split(a[:batch], chunks, axis=-1))

    def sample_mods(a, chunks):
        return tuple(jnp.tile(c, (dec_seq, 1))[None] for c in jnp.split(a[batch:batch + db], chunks, axis=-1))

    b_conv3 = b_conv.reshape(depth, 1, 2 * d_ff)
    weights = (norm_attn_g, norm_ffn_g, w_qkv_a, w_o_a, w_q_b, w_o_b, sb_bias, norm_kv_g, w_kv,
               w_up, w_conv, b_conv3, w_down, norm_final_g)

    bm_p = _pick(seq, (1024, 512, 256, 128))
    sb_blk = _pick(seq, (256, 128))
    cfg_p = dict(
        bm=bm_p, bm_norm=_pick(seq, (256, 128)), bm_down=_pick(seq, (2048, 1024, 512, 256, 128)), bn_down=512,
        rows_per_mod=seq, rows_per_seq=seq, tap=1, q_dtype=BF16,
        tab_map=lambda j, i: (i % (seq // bm_p), 0),
        attn_a=lambda qkv, l: _attn_a_prompt(qkv, batch=batch, seq=seq, heads=heads_a),
        attn_b=lambda q, kv, bias: _sb_prompt(q, kv, bias, batch=batch, seq=seq, heads=heads_b, blk=sb_blk),
    )
    y_p, qkv_p, kv_p, tails_p = _trunk(
        x_prompt.reshape(batch * seq, d),
        [prompt_mods(mod_all[l], 6) for l in range(depth)], prompt_mods(mod_kv_all, 2),
        _rope_table(jnp.arange(seq)), cfg_p, weights, None)

    win_p = []
    for g in range(N_GROUPS_A):
        keep = min(WINDOWS[g], seq)
        per_layer = [q.reshape(batch, seq, N_GROUPS_A, 3, heads_a, HEAD_DIM)[:, seq - keep:, g, 1:3]
                     for q in qkv_p]
        win_p.append(jnp.stack(per_layer, axis=0))
    kv_prompt = kv_p.reshape(batch, seq, 2, heads_b, HEAD_DIM)
    nblk = seq // bm_p
    conv_p = jnp.stack([
        jnp.concatenate([tg.reshape(batch, nblk, SUBLANES, d_ff)[:, -1, SUBLANES - (CONV_W - 1):],
                         tu.reshape(batch, nblk, SUBLANES, d_ff)[:, -1, SUBLANES - (CONV_W - 1):]], axis=-1)
        for tg, tu in tails_p], axis=0)

    m_s = dec_seq * db
    x_s = x_sample.transpose(1, 0, 2).reshape(m_s, d)
    pages = cache_kv_pages.reshape(cache_kv_pages.shape[0], page * 2 * heads_b, HEAD_DIM)
    prev = state_conv.transpose(0, 2, 1, 3).reshape(depth, (CONV_W - 1) * db, 2 * d_ff)

    def n_major(a):
        return a.reshape(dec_seq, db, a.shape[-1]).transpose(1, 0, 2)

    def t_major(a):
        return a.transpose(1, 0, 2).reshape(m_s, a.shape[-1])

    def attn_b_sample(q, kv, bias):
        kv_new = n_major(kv).reshape(db, dec_seq * 2 * heads_b, HEAD_DIM)
        kv_pad = jnp.concatenate([kv_new, jnp.zeros((db, (page - dec_seq) * 2 * heads_b, HEAD_DIM), F32)], axis=1)
        return t_major(_sb_sample(n_major(q), kv_pad, pages, page_table, bias, heads=heads_b, page=page))

    cfg_s = dict(
        bm=m_s, bm_norm=m_s, bm_down=m_s, bn_down=1024, rows_per_mod=m_s, rows_per_seq=m_s, tap=db, q_dtype=F32,
        tab_map=lambda j, i: (0, 0),
        attn_a=lambda qkv, l: t_major(_attn_a_sample(n_major(qkv), caches, l, heads=heads_a)),
        attn_b=attn_b_sample,
    )
    pos_s = past_len + jnp.repeat(jnp.arange(dec_seq), db)
    y_s, qkv_s, kv_s, tails_s = _trunk(
        x_s, [sample_mods(mod_all[l], 6) for l in range(depth)], sample_mods(mod_kv_all, 2),
        _rope_table(pos_s), cfg_s, weights, dict(prev=prev))

    qkv_new = jnp.stack([n_major(q) for q in qkv_s], axis=0).reshape(
        n_a, db, dec_seq, N_GROUPS_A, 3, heads_a, HEAD_DIM)
    win_s = [jnp.concatenate([caches[g][:, :, dec_seq:], qkv_new[:, :, :, g, 1:3]], axis=2)
             for g in range(N_GROUPS_A)]
    kv_sample = n_major(kv_s).reshape(db, dec_seq, 2, heads_b, HEAD_DIM)
    conv_s = jnp.stack([
        jnp.concatenate([tg[0], tu[0]], axis=-1).reshape(CONV_W - 1, db, 2 * d_ff).transpose(1, 0, 2)
        for tg, tu in tails_s], axis=0)

    y_prompt = y_p.reshape(batch, seq, d)
    y_sample = y_s.reshape(dec_seq, db, d).transpose(1, 0, 2)
    return (y_prompt, y_sample, win_p[0], win_s[0], win_p[1], win_s[1], win_p[2], win_s[2],
            kv_prompt, kv_sample, conv_p, conv_s)
```

```python
import functools

import jax
import jax.numpy as jnp
from jax import lax
from jax.experimental import pallas as pl
from jax.experimental.pallas import tpu as pltpu

F32 = jnp.float32
BF16 = jnp.bfloat16

HEAD_DIM = 128
N_GROUPS_A = 3
WINDOWS = (128, 512, 2048)
DILATIONS = (1, 4, 16)
ROT_DIM = HEAD_DIM // 4
ROPE_THETA = 500000.0
CONV_W = 3
RMS_EPS = 1e-6
NEG_BIG = -1e30
UNROLL_A = 4
K_BLOCK_BIG = 512
BAND = 128
SUBLANES = 8
ADA_ROWS = 16
VMEM_BIG = 56 * 1024 * 1024
VMEM_MID = 48 * 1024 * 1024
VMEM_DOWN = 44 * 1024 * 1024


def _params(sem, vmem=VMEM_MID):
    return pltpu.CompilerParams(dimension_semantics=sem, vmem_limit_bytes=vmem)


def _sigmoid(x):
    return 1.0 / (1.0 + jnp.exp(-x))


def _norm_kernel(x_ref, g_ref, sc_ref, sh_ref, o_ref):
    x = x_ref[...]
    ms = jnp.mean(x * x, axis=-1, keepdims=True)
    y = x * lax.rsqrt(ms + RMS_EPS) * g_ref[...]
    o_ref[...] = (y * (1.0 + sc_ref[...]) + sh_ref[...]).astype(o_ref.dtype)


def _final_norm_kernel(x_ref, g_ref, o_ref):
    x = x_ref[...]
    ms = jnp.mean(x * x, axis=-1, keepdims=True)
    o_ref[...] = (x * lax.rsqrt(ms + RMS_EPS) * g_ref[...]).astype(o_ref.dtype)


def _norm_mod(x, g, sc, sh, *, bm, rows_per_mod):
    m, d = x.shape
    r = sc.shape[1]
    mod_spec = pl.BlockSpec((None, r, d), lambda i: ((i * bm) // rows_per_mod, 0, 0))
    return pl.pallas_call(
        _norm_kernel,
        grid=(m // bm,),
        in_specs=[pl.BlockSpec((bm, d), lambda i: (i, 0)),
                  pl.BlockSpec((1, d), lambda i: (0, 0)),
                  mod_spec, mod_spec],
        out_specs=pl.BlockSpec((bm, d), lambda i: (i, 0)),
        out_shape=jax.ShapeDtypeStruct((m, d), BF16),
        compiler_params=_params(("arbitrary",)),
        name="norm_mod",
    )(x, g, sc, sh)


def _final_norm(x, g, *, bm):
    m, d = x.shape
    return pl.pallas_call(
        _final_norm_kernel,
        grid=(m // bm,),
        in_specs=[pl.BlockSpec((bm, d), lambda i: (i, 0)),
                  pl.BlockSpec((1, d), lambda i: (0, 0))],
        out_specs=pl.BlockSpec((bm, d), lambda i: (i, 0)),
        out_shape=jax.ShapeDtypeStruct((m, d), F32),
        compiler_params=_params(("arbitrary",)),
        name="final_norm",
    )(x, g)


def _cast_weight(w_ref, wb_ref):
    k = w_ref.shape[0]
    chunk = 512 if k % 512 == 0 else k

    def body(c, _):
        r0 = pl.multiple_of(c * chunk, chunk)
        wb_ref[pl.ds(r0, chunk), :] = w_ref[pl.ds(r0, chunk), :].astype(BF16)
        return 0

    lax.fori_loop(0, k // chunk, body, 0)


def _rope_tile(acc, tab_ref):
    cos = tab_ref[:, 0:HEAD_DIM]
    s_lo = tab_ref[:, HEAD_DIM:2 * HEAD_DIM]
    s_hi = tab_ref[:, 2 * HEAD_DIM:3 * HEAD_DIM]
    half = ROT_DIM // 2
    pieces = []
    for hh in range(acc.shape[1] // HEAD_DIM):
        x = acc[:, hh * HEAD_DIM:(hh + 1) * HEAD_DIM]
        nxt = pltpu.roll(x, HEAD_DIM - half, axis=1)
        prv = pltpu.roll(x, half, axis=1)
        pieces.append(x * cos + nxt * s_lo + prv * s_hi)
    return pieces


_N_EXTRA = {"plain": 0, "bias": 1, "resid": 2, "rope": 1}


def _mm_epilogue(acc, extra, o_ref, *, mode, qkv_width, out_scale):
    if mode == "plain":
        if out_scale != 1.0:
            acc = acc * out_scale
        o_ref[...] = acc.astype(o_ref.dtype)
    elif mode == "bias":
        o_ref[...] = (acc + extra[0][...]).astype(o_ref.dtype)
    elif mode == "resid":
        res_ref, gate_ref = extra
        o_ref[...] = (res_ref[...] + gate_ref[...] * acc).astype(o_ref.dtype)
    else:
        bn = acc.shape[1]
        kind = ((pl.program_id(0) * bn) // qkv_width) % 3

        @pl.when(kind == 2)
        def _():
            o_ref[...] = acc.astype(o_ref.dtype)

        @pl.when(kind != 2)
        def _():
            for hh, piece in enumerate(_rope_tile(acc, extra[0])):
                o_ref[:, hh * HEAD_DIM:(hh + 1) * HEAD_DIM] = piece.astype(o_ref.dtype)


def _mm_kernel(*refs, mode, silu_in, qkv_width, out_scale, side):
    ne = _N_EXTRA[mode]
    x_ref, w_ref = refs[0], refs[1]
    extra = refs[2:2 + ne]
    pos = 2 + ne
    if side:
        xs_ref, extra_s = refs[pos], refs[pos + 1:pos + 1 + ne]
        pos += 1 + ne
    o_ref = refs[pos]
    os_ref = refs[pos + 1] if side else None
    wb_ref = refs[-1]
    epilogue = functools.partial(_mm_epilogue, mode=mode, qkv_width=qkv_width, out_scale=out_scale)

    @pl.when(pl.program_id(1) == 0)
    def _():
        _cast_weight(w_ref, wb_ref)
        if side:
            acc_s = jnp.dot(xs_ref[...].astype(BF16), wb_ref[...], preferred_element_type=F32)
            epilogue(acc_s, extra_s, os_ref)

    x = x_ref[...]
    if silu_in:
        x = x * _sigmoid(x)
    epilogue(jnp.dot(x.astype(BF16), wb_ref[...], preferred_element_type=F32), extra, o_ref)


def _matmul(x, w, layer, *, bm, bn, out_dtype, mode="plain", extras=(), extra_specs=(),
            silu_in=False, qkv_width=0, out_scale=1.0, side=None, vmem=VMEM_BIG, name="matmul"):
    m, k = x.shape
    n = w.shape[2]
    kern = functools.partial(_mm_kernel, mode=mode, silu_in=silu_in, qkv_width=qkv_width, out_scale=out_scale,
                             side=side is not None)
    in_specs = [pl.BlockSpec((bm, k), lambda j, i: (i, 0)),
                pl.BlockSpec((None, k, bn), lambda j, i: (layer, 0, j)),
                *extra_specs]
    args = [x, w, *extras]
    out_specs = pl.BlockSpec((bm, bn), lambda j, i: (i, j))
    out_shape = jax.ShapeDtypeStruct((m, n), out_dtype)
    if side is not None:
        xs, extras_s, specs_s, dtype_s = side
        ms = xs.shape[0]
        in_specs += [pl.BlockSpec((ms, k), lambda j, i: (0, 0)), *specs_s]
        args += [xs, *extras_s]
        out_specs = [out_specs, pl.BlockSpec((ms, bn), lambda j, i: (0, j))]
        out_shape = [out_shape, jax.ShapeDtypeStruct((ms, n), dtype_s)]
    return pl.pallas_call(
        kern,
        grid=(n // bn, m // bm),
        in_specs=in_specs,
        out_specs=out_specs,
        out_shape=out_shape,
        scratch_shapes=[pltpu.VMEM((k, bn), BF16)],
        compiler_params=_params(("arbitrary", "arbitrary"), vmem),
        name=name,
    )(*args)


def _mm_ktiled_kernel(*refs, nk_big, nk_small, side):
    nw = 2 if nk_small else 1
    per = nw + 2
    w_refs = refs[:nw]
    groups = [refs[nw:nw + per]] + ([refs[nw + per:nw + 2 * per]] if side else [])
    outs = refs[nw + len(groups) * per:]
    kk = pl.program_id(2)
    first_rows = pl.program_id(1) == 0

    def each_group(fn):
        fn(groups[0], outs[0])
        if side:
            @pl.when(first_rows)
            def _():
                fn(groups[1], outs[1])

    @pl.when(kk == 0)
    def _():
        wb = w_refs[0][...].astype(BF16)

        def start(g, o_ref):
            o_ref[...] = jnp.dot(g[0][...], wb, preferred_element_type=F32)
        each_group(start)

    @pl.when(jnp.logical_and(kk > 0, kk < nk_big))
    def _():
        wb = w_refs[0][...].astype(BF16)

        def add(g, o_ref):
            o_ref[...] += jnp.dot(g[0][...], wb, preferred_element_type=F32)
        each_group(add)

    if nk_small:
        @pl.when(kk >= nk_big)
        def _():
            ws = w_refs[1][...].astype(BF16)

            def add_small(g, o_ref):
                o_ref[...] += jnp.dot(g[1][...], ws, preferred_element_type=F32)
            each_group(add_small)

    @pl.when(kk == nk_big + nk_small - 1)
    def _():
        def finish(g, o_ref):
            o_ref[...] = g[nw][...] + g[nw + 1][...] * o_ref[...]
        each_group(finish)


def _matmul_ktiled_resid(x, w, layer, res, gate, *, bm, bn, rows_per_mod, side=None, name="down"):
    m, k = x.shape
    n = w.shape[2]
    r = gate.shape[1]
    bk_big = min(k, K_BLOCK_BIG)
    nk_big = k // bk_big
    rem = k - nk_big * bk_big
    bk_small = _pick(rem, (256, 128)) if rem else 0
    nk_small = rem // bk_small if rem else 0
    small0 = (nk_big * bk_big) // bk_small if rem else 0
    assert rem == 0 or (nk_big * bk_big) % bk_small == 0

    def big(kk):
        return jnp.minimum(kk, nk_big - 1)

    def small(kk):
        return small0 + jnp.maximum(kk - nk_big, 0)

    in_specs = [pl.BlockSpec((None, bk_big, bn), lambda j, i, kk: (layer, big(kk), j))]
    args = [w]
    if nk_small:
        in_specs.append(pl.BlockSpec((None, bk_small, bn), lambda j, i, kk: (layer, small(kk), j)))
        args.append(w)
    in_specs.append(pl.BlockSpec((bm, bk_big), lambda j, i, kk: (i, big(kk))))
    args.append(x)
    if nk_small:
        in_specs.append(pl.BlockSpec((bm, bk_small), lambda j, i, kk: (i, small(kk))))
        args.append(x)
    in_specs += [pl.BlockSpec((bm, bn), lambda j, i, kk: (i, j), pipeline_mode=pl.Buffered(1)),
                 pl.BlockSpec((None, r, bn), lambda j, i, kk: ((i * bm) // rows_per_mod, 0, j))]
    args += [res, gate]
    out_specs = pl.BlockSpec((bm, bn), lambda j, i, kk: (i, j))
    out_shape = jax.ShapeDtypeStruct((m, n), F32)
    if side is not None:
        xs, res_s, gate_s = side
        ms = xs.shape[0]
        in_specs.append(pl.BlockSpec((ms, bk_big), lambda j, i, kk: (0, big(kk))))
        args.append(xs)
        if nk_small:
            in_specs.append(pl.BlockSpec((ms, bk_small), lambda j, i, kk: (0, small(kk))))
            args.append(xs)
        in_specs += [pl.BlockSpec((ms, bn), lambda j, i, kk: (0, j)),
                     pl.BlockSpec((None, ms, bn), lambda j, i, kk: (0, 0, j))]
        args += [res_s, gate_s]
        out_specs = [out_specs, pl.BlockSpec((ms, bn), lambda j, i, kk: (0, j))]
        out_shape = [out_shape, jax.ShapeDtypeStruct((ms, n), F32)]
    return pl.pallas_call(
        functools.partial(_mm_ktiled_kernel, nk_big=nk_big, nk_small=nk_small, side=side is not None),
        grid=(n // bn, m // bm, nk_big + nk_small),
        in_specs=in_specs,
        out_specs=out_specs,
        out_shape=out_shape,
        compiler_params=_params(("arbitrary", "arbitrary", "arbitrary"), VMEM_DOWN),
        name=name,
    )(*args)


def _ffn_rows(h_ref, act_ref, stg_ref, stu_ref, ug, uu, wgb, wub, conv_refs, *, rows, halo, tap):
    cwg_ref, cwu_ref, cbg_ref, cbu_ref = conv_refs
    h = h_ref[...]
    ug[halo:halo + rows, :] = jnp.dot(h, wgb[...], preferred_element_type=F32)
    uu[halo:halo + rows, :] = jnp.dot(h, wub[...], preferred_element_type=F32)

    def conv(u, cw_ref, cb_ref):
        y = cb_ref[...] + cw_ref[0:1, :] * u[halo - 2 * tap:halo - 2 * tap + rows, :]
        y = y + cw_ref[1:2, :] * u[halo - tap:halo - tap + rows, :]
        return y + cw_ref[2:3, :] * u[halo:halo + rows, :]

    gate = conv(ug, cwg_ref, cbg_ref)
    up = conv(uu, cwu_ref, cbu_ref)
    act_ref[...] = (gate * _sigmoid(gate) * up).astype(act_ref.dtype)
    tail_g = ug[rows:rows + halo, :]
    tail_u = uu[rows:rows + halo, :]
    stg_ref[...] = tail_g
    stu_ref[...] = tail_u
    ug[0:halo, :] = tail_g
    uu[0:halo, :] = tail_u


def _ffn_up_kernel(*refs, bm, halo, blocks_per_seq, side):
    h_ref, wg_ref, wu_ref = refs[0:3]
    conv_refs = refs[3:7]
    pos = 7
    if side:
        hs_ref, pgs_ref, pus_ref = refs[pos:pos + 3]
        pos += 3
    act_ref, stg_ref, stu_ref = refs[pos:pos + 3]
    pos += 3
    if side:
        acts_ref, stgs_ref, stus_ref = refs[pos:pos + 3]
        pos += 3
    wgb, wub, ug, uu = refs[pos:pos + 4]
    i = pl.program_id(1)

    @pl.when(i == 0)
    def _():
        _cast_weight(wg_ref, wgb)
        _cast_weight(wu_ref, wub)
        if side:
            halo_s, tap_s = side
            ugs, uus = refs[pos + 4:pos + 6]
            ugs[0:halo_s, :] = pgs_ref[...]
            uus[0:halo_s, :] = pus_ref[...]
            _ffn_rows(hs_ref, acts_ref, stgs_ref, stus_ref, ugs, uus, wgb, wub, conv_refs,
                      rows=hs_ref.shape[0], halo=halo_s, tap=tap_s)

    @pl.when(i % blocks_per_seq == 0)
    def _():
        ug[0:halo, :] = jnp.zeros((halo, ug.shape[1]), F32)
        uu[0:halo, :] = jnp.zeros((halo, uu.shape[1]), F32)

    _ffn_rows(h_ref, act_ref, stg_ref, stu_ref, ug, uu, wgb, wub, conv_refs, rows=bm, halo=halo, tap=1)


def _ffn_up(h, w_up, w_conv, b_conv3, layer, *, bm, bn, rows_per_seq, side=None):
    m, d = h.shape
    d_ff = w_up.shape[2] // 2
    nj = d_ff // bn
    halo = SUBLANES
    in_specs = [pl.BlockSpec((bm, d), lambda j, i: (i, 0)),
                pl.BlockSpec((None, d, bn), lambda j, i: (layer, 0, j)),
                pl.BlockSpec((None, d, bn), lambda j, i: (layer, 0, j + nj)),
                pl.BlockSpec((None, CONV_W, bn), lambda j, i: (layer, 0, j)),
                pl.BlockSpec((None, CONV_W, bn), lambda j, i: (layer, 0, j + nj)),
                pl.BlockSpec((None, 1, bn), lambda j, i: (layer, 0, j)),
                pl.BlockSpec((None, 1, bn), lambda j, i: (layer, 0, j + nj))]
    args = [h, w_up, w_up, w_conv, w_conv, b_conv3, b_conv3]
    st_spec = pl.BlockSpec((None, halo, bn), lambda j, i: (i, 0, j))
    out_specs = [pl.BlockSpec((bm, bn), lambda j, i: (i, j)), st_spec, st_spec]
    out_shape = [jax.ShapeDtypeStruct((m, d_ff), BF16),
                 jax.ShapeDtypeStruct((m // bm, halo, d_ff), F32),
                 jax.ShapeDtypeStruct((m // bm, halo, d_ff), F32)]
    scratch = [pltpu.VMEM((d, bn), BF16), pltpu.VMEM((d, bn), BF16),
               pltpu.VMEM((bm + halo, bn), F32), pltpu.VMEM((bm + halo, bn), F32)]
    side_cfg = None
    if side is not None:
        hs, prev, tap_s = side
        ms = hs.shape[0]
        halo_s = (CONV_W - 1) * tap_s
        side_cfg = (halo_s, tap_s)
        in_specs += [pl.BlockSpec((ms, d), lambda j, i: (0, 0)),
                     pl.BlockSpec((halo_s, bn), lambda j, i: (0, j)),
                     pl.BlockSpec((halo_s, bn), lambda j, i: (0, j + nj))]
        args += [hs, prev, prev]
        sts_spec = pl.BlockSpec((halo_s, bn), lambda j, i: (0, j))
        out_specs += [pl.BlockSpec((ms, bn), lambda j, i: (0, j)), sts_spec, sts_spec]
        out_shape += [jax.ShapeDtypeStruct((ms, d_ff), BF16),
                      jax.ShapeDtypeStruct((halo_s, d_ff), F32), jax.ShapeDtypeStruct((halo_s, d_ff), F32)]
        scratch += [pltpu.VMEM((ms + halo_s, bn), F32), pltpu.VMEM((ms + halo_s, bn), F32)]
    kern = functools.partial(_ffn_up_kernel, bm=bm, halo=halo, blocks_per_seq=rows_per_seq // bm, side=side_cfg)
    return pl.pallas_call(
        kern,
        grid=(nj, m // bm),
        in_specs=in_specs,
        out_specs=out_specs,
        out_shape=out_shape,
        scratch_shapes=scratch,
        compiler_params=_params(("arbitrary", "arbitrary"), VMEM_BIG),
        name="ffn_up",
    )(*args)


def _attn_a_prompt_kernel(q0, k0, v0, q1, k1, v1, q2, k2, v2, o_ref, og, lg, *, seq):
    qkv = ((q0, k0, v0), (q1, k1, v1), (q2, k2, v2))
    scale = HEAD_DIM ** -0.5
    row = lax.broadcasted_iota(jnp.int32, (BAND, 2 * BAND), 0)
    col = lax.broadcasted_iota(jnp.int32, (BAND, 2 * BAND), 1)
    cur_ok = jnp.logical_and(col >= BAND, col - BAND <= row)
    prev_ok = jnp.logical_and(col < BAND, col >= row)
    dn = (((1,), (1,)), ((), ()))

    for g in range(N_GROUPS_A):
        dil = DILATIONS[g]
        nb = seq // dil // BAND
        q_ref, k_ref, v_ref = qkv[g]
        unroll = UNROLL_A if (dil * nb) % UNROLL_A == 0 else 1

        def rows(start, dil=dil):
            if dil == 1:
                return pl.ds(pl.multiple_of(start, BAND), BAND)
            return pl.ds(start, BAND, stride=dil)

        def body(it, _, g=g, dil=dil, nb=nb, q_ref=q_ref, k_ref=k_ref, v_ref=v_ref, rows=rows, unroll=unroll):
            starts, oks, scores, vbands = [], [], [], []
            for u in range(unroll):
                idx = it * unroll + u
                blk = idx % nb
                start = idx // nb + blk * (BAND * dil)
                has_prev = blk > 0
                pstart = jnp.where(has_prev, start - BAND * dil, start)
                q = q_ref[rows(start), :].astype(BF16)
                kband = jnp.concatenate([k_ref[rows(pstart), :], k_ref[rows(start), :]], axis=0).astype(BF16)
                vbands.append(jnp.concatenate([v_ref[rows(pstart), :], v_ref[rows(start), :]], axis=0).astype(BF16))
                scores.append(lax.dot_general(q, kband, dn, preferred_element_type=F32))
                starts.append(start)
                oks.append(jnp.logical_or(cur_ok, jnp.logical_and(prev_ok, has_prev)))
            probs, dens, lses = [], [], []
            for u in range(unroll):
                sc = jnp.where(oks[u], scores[u] * scale, NEG_BIG)
                mx = jnp.max(sc, axis=1, keepdims=True)
                p = jnp.exp(sc - mx)
                den = jnp.sum(p, axis=1, keepdims=True)
                probs.append(p.astype(BF16))
                dens.append(den)
                lses.append(mx + jnp.log(den))
            for u in range(unroll):
                o = jnp.dot(probs[u], vbands[u], preferred_element_type=F32)
                og[g, rows(starts[u]), :] = o / dens[u]
                lg[g, rows(starts[u]), :] = jnp.broadcast_to(lses[u], (BAND, HEAD_DIM))
            return 0

        lax.fori_loop(0, dil * nb // unroll, body, 0)

    chunk = 256

    def comb(c, _):
        r0 = pl.multiple_of(c * chunk, chunk)
        l0 = lg[0, pl.ds(r0, chunk), :]
        l1 = lg[1, pl.ds(r0, chunk), :]
        l2 = lg[2, pl.ds(r0, chunk), :]
        mx = jnp.maximum(jnp.maximum(l0, l1), l2)
        e0 = jnp.exp(l0 - mx)
        e1 = jnp.exp(l1 - mx)
        e2 = jnp.exp(l2 - mx)
        tot = e0 + e1 + e2
        o = (og[0, pl.ds(r0, chunk), :] * (e0 / tot) + og[1, pl.ds(r0, chunk), :] * (e1 / tot)
             + og[2, pl.ds(r0, chunk), :] * (e2 / tot))
        o_ref[pl.ds(r0, chunk), :] = o.astype(o_ref.dtype)
        return 0

    lax.fori_loop(0, seq // chunk, comb, 0)


def _attn_a_prompt(qkv, *, batch, seq, heads):
    m = qkv.shape[0]
    specs = []
    for g in range(N_GROUPS_A):
        for t in range(3):
            specs.append(pl.BlockSpec((seq, HEAD_DIM),
                                      lambda n, h, g=g, t=t: (n, (g * 3 + t) * heads + h)))
    return pl.pallas_call(
        functools.partial(_attn_a_prompt_kernel, seq=seq),
        grid=(batch, heads),
        in_specs=specs,
        out_specs=pl.BlockSpec((seq, HEAD_DIM), lambda n, h: (n, h)),
        out_shape=jax.ShapeDtypeStruct((m, heads * HEAD_DIM), BF16),
        scratch_shapes=[pltpu.VMEM((N_GROUPS_A, seq, HEAD_DIM), F32),
                        pltpu.VMEM((N_GROUPS_A, seq, HEAD_DIM), F32)],
        compiler_params=_params(("arbitrary", "arbitrary")),
        name="attn_a_prompt",
    )(*([qkv] * 9))


def _attn_a_sample_kernel(qn_ref, c0_ref, c1_ref, c2_ref, o_ref, *, heads, dec_seq):
    width = heads * HEAD_DIM
    scale = HEAD_DIM ** -0.5
    caches = (c0_ref, c1_ref, c2_ref)
    hrow = lax.broadcasted_iota(jnp.int32, (heads, width), 0)
    hcol = lax.broadcasted_iota(jnp.int32, (heads, width), 1) // HEAD_DIM
    head_mask = hrow == hcol
    key_idx = lax.broadcasted_iota(jnp.int32, (heads, BAND), 1)
    dn = (((1,), (1,)), ((), ()))

    def split_heads(x):
        y = pltpu.einshape("mrd->rmd", x.astype(BF16))
        k = jnp.concatenate([y[h] for h in range(heads)], axis=1)
        v = jnp.concatenate([y[heads + h] for h in range(heads)], axis=1)
        return k, v

    rh = 2 * heads
    kv_bufs = [[split_heads(caches[0][...])]]
    for g in range(1, N_GROUPS_A):
        kv_bufs.append([split_heads(caches[g][:, t * rh:(t + 1) * rh, :]) for t in range(dec_seq)])

    for t in range(dec_seq):
        outs, lses = [], []
        for g in range(N_GROUPS_A):
            base = g * 3 * width
            q_row = qn_ref[t:t + 1, base:base + width]
            qbd = jnp.where(head_mask, jnp.broadcast_to(q_row, (heads, width)), 0.0)
            kb, vb = kv_bufs[g][0 if g == 0 else t]
            s_buf = lax.dot_general(qbd.astype(BF16), kb, dn, preferred_element_type=F32) * scale
            if g == 0:
                s_buf = jnp.where(key_idx >= t, s_buf, NEG_BIG)
                new_rows = range(t + 1)
            else:
                new_rows = (t,)
            s_new = []
            for tn in new_rows:
                k_row = qn_ref[tn:tn + 1, base + width:base + 2 * width]
                s_new.append(jnp.sum(qbd * k_row, axis=1, keepdims=True) * scale)
            mx = jnp.max(s_buf, axis=1, keepdims=True)
            for s in s_new:
                mx = jnp.maximum(mx, s)
            p_buf = jnp.exp(s_buf - mx)
            den = jnp.sum(p_buf, axis=1, keepdims=True)
            o = jnp.dot(p_buf.astype(BF16), vb, preferred_element_type=F32)
            for tn, s in zip(new_rows, s_new):
                p = jnp.exp(s - mx)
                den = den + p
                o = o + p * qn_ref[tn:tn + 1, base + 2 * width:base + 3 * width]
            o = jnp.where(head_mask, o / den, 0.0)
            lse = jnp.where(head_mask, jnp.broadcast_to(mx + jnp.log(den), (heads, width)), 0.0)
            outs.append(jnp.sum(o, axis=0, keepdims=True))
            lses.append(jnp.sum(lse, axis=0, keepdims=True))
        mx = jnp.maximum(jnp.maximum(lses[0], lses[1]), lses[2])
        es = [jnp.exp(l - mx) for l in lses]
        tot = es[0] + es[1] + es[2]
        row = outs[0] * (es[0] / tot) + outs[1] * (es[1] / tot) + outs[2] * (es[2] / tot)
        o_ref[t:t + 1, :] = row.astype(o_ref.dtype)


def _attn_a_sample(qn, caches, layer, *, heads):
    db, dec_seq, _ = qn.shape
    width = heads * HEAD_DIM
    views, specs = [], []
    rh = 2 * heads
    for g in range(N_GROUPS_A):
        c = caches[g]
        nlay, _, nbuf = c.shape[:3]
        dil = DILATIONS[g]
        views.append(c.reshape(nlay, db, nbuf // dil, dil * rh, HEAD_DIM))
        need = rh if g == 0 else dec_seq * rh
        blk_r = need if need == dil * rh else -(-need // SUBLANES) * SUBLANES
        specs.append(pl.BlockSpec((None, None, nbuf // dil, blk_r, HEAD_DIM), lambda n: (layer, n, 0, 0, 0)))
    return pl.pallas_call(
        functools.partial(_attn_a_sample_kernel, heads=heads, dec_seq=dec_seq),
        grid=(db,),
        in_specs=[pl.BlockSpec((None, dec_seq, qn.shape[2]), lambda n: (n, 0, 0)), *specs],
        out_specs=pl.BlockSpec((None, dec_seq, width), lambda n: (n, 0, 0)),
        out_shape=jax.ShapeDtypeStruct((db, dec_seq, width), F32),
        compiler_params=_params(("arbitrary",)),
        name="attn_a_sample",
    )(qn, *views)


LOG2E = 1.4426950408889634
LN2 = 0.6931471805599453
SB_QSCALE = HEAD_DIM ** -0.5 * LOG2E


def _sb_blocks(x2s, tri, masks, vs):
    sps = []
    for x2, mask in zip(x2s, masks):
        neg_abs = pltpu.bitcast(pltpu.bitcast(x2, jnp.uint32) | jnp.uint32(0x80000000), F32)
        sp = LN2 * jnp.maximum(x2, 0.0) + jnp.log(1.0 + jnp.exp2(neg_abs))
        sps.append(sp if mask is None else jnp.where(mask, sp, 0.0))
    incls = [jnp.dot(sp.astype(BF16), tri, preferred_element_type=F32) for sp in sps]
    ws = []
    for x2, incl, mask in zip(x2s, incls, masks):
        a = jnp.exp2(x2 - LOG2E * incl)
        ws.append((a if mask is None else jnp.where(mask, a, 0.0)).astype(BF16))
    pvs = [jnp.dot(a, v, preferred_element_type=F32) for a, v in zip(ws, vs)]
    return [(pv, jnp.sum(sp, axis=1, keepdims=True)) for pv, sp in zip(pvs, sps)]


def _sb_prompt_kernel(q_ref, k_ref, v_ref, b_ref, tri_ref, o_ref, kb_ref, vb_ref, *, seq, blk, hpb):
    nblk = seq // blk
    chunk = 512 if seq % 512 == 0 else seq

    def cast(c, _):
        r0 = pl.multiple_of(c * chunk, chunk)
        kb_ref[pl.ds(r0, chunk), :] = k_ref[pl.ds(r0, chunk), :].astype(BF16)
        vb_ref[pl.ds(r0, chunk), :] = v_ref[pl.ds(r0, chunk), :].astype(BF16)
        return 0

    lax.fori_loop(0, seq // chunk, cast, 0)
    row = lax.broadcasted_iota(jnp.int32, (blk, blk), 0)
    col = lax.broadcasted_iota(jnp.int32, (blk, blk), 1)
    diag_mask = col < row
    dn = (((1,), (1,)), ((), ()))
    lanes = [slice(hh * HEAD_DIM, (hh + 1) * HEAD_DIM) for hh in range(hpb)]

    def q_block(qb, _):
        q0 = pl.multiple_of(qb * blk, blk)
        qs = [q_ref[pl.ds(q0, blk), ln] for ln in lanes]

        def blocks(kbs, mask):
            k0s = [pl.multiple_of(kb * blk, blk) for kb in kbs]
            x2s = [lax.dot_general(qs[hh], kb_ref[pl.ds(k0, blk), lanes[hh]], dn,
                                   preferred_element_type=F32) + b_ref[hh]
                   for hh in range(hpb) for k0 in k0s]
            vs = [vb_ref[pl.ds(k0, blk), lanes[hh]] for hh in range(hpb) for k0 in k0s]
            out = _sb_blocks(x2s, tri_ref[...], [mask] * len(x2s), vs)
            return [out[hh * len(kbs):(hh + 1) * len(kbs)] for hh in range(hpb)]

        state = tuple(res[0] for res in blocks([qb], diag_mask))

        def pair(it, state):
            res = blocks([qb - 1 - 2 * it, qb - 2 - 2 * it], None)
            out = []
            for (acc, carry), ((pv0, m0), (pv1, m1)) in zip(state, res):
                acc = acc + jnp.exp(-carry) * pv0 + jnp.exp(-(carry + m0)) * pv1
                out.append((acc, carry + m0 + m1))
            return tuple(out)

        state = lax.fori_loop(0, qb // 2, pair, state)

        def last(state):
            res = blocks([0], None)
            return tuple((acc + jnp.exp(-carry) * pv, carry + m)
                         for (acc, carry), ((pv, m),) in zip(state, res))

        state = lax.cond(qb % 2 == 1, last, lambda st: st, state)
        for hh, (acc, _) in enumerate(state):
            o_ref[pl.ds(q0, blk), lanes[hh]] = acc.astype(o_ref.dtype)
        return 0

    lax.fori_loop(0, nblk, q_block, 0)


def _tri(blk):
    r = lax.broadcasted_iota(jnp.int32, (blk, blk), 0)
    c = lax.broadcasted_iota(jnp.int32, (blk, blk), 1)
    return (r >= c).astype(BF16)


def _sb_prompt(q, kv, bias, *, batch, seq, heads, blk, hpb=4):
    m = q.shape[0]
    hpb = hpb if heads % hpb == 0 else 1
    w = hpb * HEAD_DIM
    bias_b = jnp.broadcast_to(LOG2E * bias.astype(F32)[:, None, None], (heads, 1, blk))
    return pl.pallas_call(
        functools.partial(_sb_prompt_kernel, seq=seq, blk=blk, hpb=hpb),
        grid=(batch, heads // hpb),
        in_specs=[pl.BlockSpec((seq, w), lambda n, h: (n, h)),
                  pl.BlockSpec((seq, w), lambda n, h: (n, h)),
                  pl.BlockSpec((seq, w), lambda n, h: (n, heads // hpb + h)),
                  pl.BlockSpec((hpb, 1, blk), lambda n, h: (h, 0, 0)),
                  pl.BlockSpec((blk, blk), lambda n, h: (0, 0))],
        out_specs=pl.BlockSpec((seq, w), lambda n, h: (n, h)),
        out_shape=jax.ShapeDtypeStruct((m, heads * HEAD_DIM), BF16),
        scratch_shapes=[pltpu.VMEM((seq, w), BF16), pltpu.VMEM((seq, w), BF16)],
        compiler_params=_params(("arbitrary", "arbitrary")),
        name="sb_prompt",
    )(q, kv, kv, bias_b, _tri(blk))


QROWS = SUBLANES


def _sb_sample_kernel(pt_ref, q_ref, new_ref, page_ref, b_ref, tri_ref, o_ref,
                      q3_ref, acc_ref, carry_ref, *, heads, dec_seq, page):
    del pt_ref
    p = pl.program_id(1)
    rows = heads * QROWS

    def process(kv_ref, mask):
        x = kv_ref[...].astype(BF16).reshape(page, 2 * heads, HEAD_DIM)
        k = pltpu.einshape("khd->hkd", x[:, 0:heads, :])
        v = pltpu.einshape("khd->hkd", x[:, heads:2 * heads, :])
        s = jnp.einsum("htd,hkd->htk", q3_ref[...].astype(BF16), k, preferred_element_type=F32)
        x2 = s.reshape(rows, page) + b_ref[...]
        neg_abs = pltpu.bitcast(pltpu.bitcast(x2, jnp.uint32) | jnp.uint32(0x80000000), F32)
        sp = LN2 * jnp.maximum(x2, 0.0) + jnp.log(1.0 + jnp.exp2(neg_abs))
        if mask is not None:
            sp = jnp.where(mask, sp, 0.0)
        incl = jnp.dot(sp.astype(BF16), tri_ref[...], preferred_element_type=F32)
        a = jnp.exp2(x2 - LOG2E * incl)
        if mask is not None:
            a = jnp.where(mask, a, 0.0)
        pv = jnp.einsum("htk,hkd->htd", a.astype(BF16).reshape(heads, QROWS, page), v,
                        preferred_element_type=F32).reshape(rows, HEAD_DIM)
        carry = carry_ref[:, 0:1]
        acc_ref[...] += jnp.exp(-carry) * pv
        carry_ref[...] = jnp.broadcast_to(carry + jnp.sum(sp, axis=1, keepdims=True), carry_ref.shape)

    @pl.when(p == 0)
    def _():
        q3_ref[...] = jnp.zeros_like(q3_ref)
        for h in range(heads):
            q3_ref[h, 0:dec_seq, :] = q_ref[:, h * HEAD_DIM:(h + 1) * HEAD_DIM]
        acc_ref[...] = jnp.zeros_like(acc_ref)
        carry_ref[...] = jnp.zeros_like(carry_ref)
        t_of_row = lax.broadcasted_iota(jnp.int32, (rows, page), 0) % QROWS
        key = lax.broadcasted_iota(jnp.int32, (rows, page), 1)
        process(new_ref, key < t_of_row)

    @pl.when(p > 0)
    def _():
        process(page_ref, None)

    @pl.when(p == pl.num_programs(1) - 1)
    def _():
        for h in range(heads):
            o_ref[:, h * HEAD_DIM:(h + 1) * HEAD_DIM] = acc_ref[h * QROWS:h * QROWS + dec_seq, :]


def _sb_sample(q, kv_new_pad, pages, page_table, bias, *, heads, page):
    db, dec_seq, width = q.shape
    assert dec_seq <= QROWS
    n_pages = page_table.shape[1]
    rows = heads * QROWS
    prow = page * 2 * heads
    bias_b = jnp.broadcast_to(LOG2E * jnp.repeat(bias.astype(F32), QROWS)[:, None], (rows, page))

    def page_map(n, p, pt):
        return (pt[n, n_pages - jnp.maximum(p, 1)], 0, 0)

    grid_spec = pltpu.PrefetchScalarGridSpec(
        num_scalar_prefetch=1,
        grid=(db, n_pages + 1),
        in_specs=[pl.BlockSpec((None, dec_seq, width), lambda n, p, pt: (n, 0, 0)),
                  pl.BlockSpec((None, prow, HEAD_DIM), lambda n, p, pt: (n, 0, 0)),
                  pl.BlockSpec((None, prow, HEAD_DIM), page_map),
                  pl.BlockSpec((rows, page), lambda n, p, pt: (0, 0)),
                  pl.BlockSpec((page, page), lambda n, p, pt: (0, 0))],
        out_specs=pl.BlockSpec((None, dec_seq, width), lambda n, p, pt: (n, 0, 0)),
        scratch_shapes=[pltpu.VMEM((heads, QROWS, HEAD_DIM), F32), pltpu.VMEM((rows, HEAD_DIM), F32),
                        pltpu.VMEM((rows, HEAD_DIM), F32)],
    )
    return pl.pallas_call(
        functools.partial(_sb_sample_kernel, heads=heads, dec_seq=dec_seq, page=page),
        grid_spec=grid_spec,
        out_shape=jax.ShapeDtypeStruct((db, dec_seq, width), F32),
        compiler_params=_params(("arbitrary", "arbitrary")),
        name="sb_sample",
    )(page_table, q, kv_new_pad, pages, bias_b, _tri(page))


def _shift_kernel(*refs, n_groups, n_rows, n_new):
    caches, news, outs = (refs[k * n_groups:(k + 1) * n_groups] for k in range(3))
    sem = refs[3 * n_groups]
    copies = []
    for g in range(n_groups):
        total = caches[g].shape[1]
        keep = total - n_new[g]
        for r in range(n_rows):
            copies.append(pltpu.make_async_copy(caches[g].at[r, pl.ds(n_new[g], keep)],
                                                outs[g].at[r, pl.ds(0, keep)], sem.at[len(copies)]))
            copies.append(pltpu.make_async_copy(news[g].at[r], outs[g].at[r, pl.ds(keep, n_new[g])],
                                                sem.at[len(copies)]))
    for c in copies:
        c.start()
    for c in copies:
        c.wait()


def _shift_caches(caches, news):
    n_groups = len(caches)
    n_rows = caches[0].shape[0]
    any_spec = pl.BlockSpec(memory_space=pl.ANY)
    return pl.pallas_call(
        functools.partial(_shift_kernel, n_groups=n_groups, n_rows=n_rows,
                          n_new=tuple(n.shape[1] for n in news)),
        in_specs=[any_spec] * (2 * n_groups),
        out_specs=[any_spec] * n_groups,
        out_shape=[jax.ShapeDtypeStruct(c.shape, c.dtype) for c in caches],
        scratch_shapes=[pltpu.SemaphoreType.DMA((2 * n_groups * n_rows,))],
        name="shift_caches",
    )(*caches, *news)


def _rope_table(pos):
    half = ROT_DIM // 2
    inv = ROPE_THETA ** (-jnp.arange(half, dtype=F32) / half)
    ang = pos.astype(F32)[:, None] * inv[None, :]
    cos, sin = jnp.cos(ang), jnp.sin(ang)
    t = pos.shape[0]
    ones = jnp.ones((t, HEAD_DIM - ROT_DIM), F32)
    zeros = jnp.zeros((t, HEAD_DIM - ROT_DIM), F32)
    zh = jnp.zeros((t, half), F32)
    return jnp.concatenate([cos, cos, ones, -sin, zh, zeros, zh, sin, zeros], axis=1)


def _pick(v, cands):
    for c in cands:
        if v % c == 0:
            return c
    return v


def _trunk(xp, xs, mods_p, mods_s, modkv_p, modkv_s, tabs_p, tabs_s, prev, cfg, weights):
    (norm_attn_g, norm_ffn_g, w_qkv_a, w_o_a, w_q_b, w_o_b, sb_bias, norm_kv_g, w_kv,
     w_up, w_conv, b_conv3, w_down, norm_final_g) = weights
    d = xp.shape[1]
    ms = xs.shape[0]
    bm, bmn, seq = cfg["bm"], cfg["bm_norm"], cfg["seq"]
    depth = w_up.shape[0]
    n_a = w_qkv_a.shape[0]
    heads_a = w_o_a.shape[1] // HEAD_DIM
    d_ff = w_down.shape[1]

    def norm_both(g, sc_p, sh_p, sc_s, sh_s):
        return (_norm_mod(xp, g, sc_p, sh_p, bm=bmn, rows_per_mod=seq),
                _norm_mod(xs, g, sc_s, sh_s, bm=ms, rows_per_mod=ms))

    def tile(n, cands=(512, 256, 128)):
        return _pick(n, cands)

    def side_rows(bn):
        return (pl.BlockSpec((ms, bn), lambda j, i: (0, j)), pl.BlockSpec((None, ms, bn), lambda j, i: (0, 0, j)))

    qkv_p, qkv_s, tails_p, tails_s = [], [], [], []
    kv_p = kv_s = None
    for l in range(depth):
        sh1, sc1, g1, sh2, sc2, g2 = mods_p[l]
        sh1s, sc1s, g1s, sh2s, sc2s, g2s = mods_s[l]
        hp, hs = norm_both(norm_attn_g[l][None], sc1, sh1, sc1s, sh1s)
        if l < n_a:
            bn = tile(heads_a * HEAD_DIM)
            qp, qs = _matmul(hp, w_qkv_a, l, bm=bm, bn=bn, out_dtype=F32, mode="rope",
                             extras=(tabs_p,), extra_specs=(pl.BlockSpec((bm, 3 * HEAD_DIM), cfg["tab_map"]),),
                             side=(hs, (tabs_s,), (pl.BlockSpec((ms, 3 * HEAD_DIM), lambda j, i: (0, 0)),), F32),
                             qkv_width=heads_a * HEAD_DIM, name="qkv")
            qkv_p.append(qp)
            qkv_s.append(qs)
            op, os_ = cfg["attn_a_p"](qp), cfg["attn_a_s"](qs, l)
            w_o, lo = w_o_a, l
        else:
            j = l - n_a
            qp, qs = _matmul(hp, w_q_b, j, bm=bm, bn=tile(w_q_b.shape[2]), out_dtype=BF16,
                             out_scale=SB_QSCALE, side=(hs, (), (), F32), name="q_b")
            op, os_ = cfg["attn_b_p"](qp, kv_p, sb_bias[j]), cfg["attn_b_s"](qs, kv_s, sb_bias[j])
            w_o, lo = w_o_b, j
        bn = tile(d)
        xp, xs = _matmul(op, w_o, lo, bm=bm, bn=bn, out_dtype=F32, mode="resid", extras=(xp, g1),
                         extra_specs=(pl.BlockSpec((bm, bn), lambda j, i: (i, j)),
                                      pl.BlockSpec((None, 1, bn), lambda j, i: ((i * bm) // seq, 0, j))),
                         side=(os_, (xs, g1s), side_rows(bn), F32), name="attn_out")
        hp, hs = norm_both(norm_ffn_g[l][None], sc2, sh2, sc2s, sh2s)
        act_p, tg, tu, act_s, tgs, tus = _ffn_up(hp, w_up, w_conv, b_conv3, l, bm=bm, bn=_pick(d_ff, (256, 128)),
                                                 rows_per_seq=seq, side=(hs, prev[l], cfg["tap_s"]))
        tails_p.append((tg, tu))
        tails_s.append((tgs, tus))
        xp, xs = _matmul_ktiled_resid(act_p, w_down, l, xp, g2, bm=cfg["bm_down"],
                                      bn=_pick(d, (cfg["bn_down"], 512, 256, 128)), rows_per_mod=seq,
                                      side=(act_s, xs, g2s))
        if l == n_a - 1:
            hp, hs = norm_both(norm_kv_g[None], modkv_p[1], modkv_p[0], modkv_s[1], modkv_s[0])
            kv_p, kv_s = _matmul(hp, w_kv[None], 0, bm=bm, bn=tile(w_kv.shape[1]), out_dtype=F32,
                                 side=(hs, (), (), F32), name="kv")
    y_p = _final_norm(xp, norm_final_g[None], bm=bmn)
    y_s = _final_norm(xs, norm_final_g[None], bm=ms)
    return (y_p, qkv_p, kv_p, tails_p), (y_s, qkv_s, kv_s, tails_s)


def kernel(x_prompt, x_sample, cache_win0, cache_win1, cache_win2, cache_kv_pages, state_conv, page_table,
           c_prompt, c_sample, norm_attn_g, norm_ffn_g, w_ada, b_ada, w_qkv_a, w_o_a, w_q_b, w_o_b, sb_bias,
           norm_kv_g, w_ada_kv, b_ada_kv, w_kv, w_up, w_conv, b_conv, w_down, norm_final_g):
    batch, seq, d = x_prompt.shape
    db, dec_seq, _ = x_sample.shape
    depth = w_up.shape[0]
    n_a = w_qkv_a.shape[0]
    heads_a = w_o_a.shape[1] // HEAD_DIM
    heads_b = w_q_b.shape[2] // HEAD_DIM
    d_ff = w_down.shape[1]
    page = cache_kv_pages.shape[1]
    past_len = page_table.shape[1] * page
    caches = (cache_win0, cache_win1, cache_win2)
    assert seq % (BAND * DILATIONS[-1]) == 0 and batch + db <= ADA_ROWS and db == SUBLANES
    assert all(caches[g].shape[2] == WINDOWS[g] for g in range(N_GROUPS_A)) and dec_seq <= DILATIONS[1]

    c_all = jnp.concatenate([c_prompt, c_sample, jnp.zeros((ADA_ROWS - batch - db, d), F32)], axis=0)

    def ada(w, b, layer):
        n = w.shape[2]
        bn = _pick(n, (1024, 512, 256, 128))
        return _matmul(c_all, w, layer, bm=ADA_ROWS, bn=bn, out_dtype=F32, mode="bias", silu_in=True,
                       extras=(b.reshape(w.shape[0], 1, n),),
                       extra_specs=(pl.BlockSpec((None, 1, bn), lambda j, i: (layer, 0, j)),), name="ada")

    mod_all = [ada(w_ada, b_ada, l) for l in range(depth)]
    mod_kv_all = ada(w_ada_kv[None], b_ada_kv[None], 0)

    def prompt_mods(a, chunks):
        return tuple(c.reshape(batch, 1, d) for c in jnp.split(a[:batch], chunks, axis=-1))

    def sample_mods(a, chunks):
        return tuple(jnp.tile(c, (dec_seq, 1))[None] for c in jnp.split(a[batch:batch + db], chunks, axis=-1))

    b_conv3 = b_conv.reshape(depth, 1, 2 * d_ff)
    weights = (norm_attn_g, norm_ffn_g, w_qkv_a, w_o_a, w_q_b, w_o_b, sb_bias, norm_kv_g, w_kv,
               w_up, w_conv, b_conv3, w_down, norm_final_g)

    m_s = dec_seq * db
    x_s = x_sample.transpose(1, 0, 2).reshape(m_s, d)
    pages = cache_kv_pages.reshape(cache_kv_pages.shape[0], page * 2 * heads_b, HEAD_DIM)
    prev = state_conv.transpose(0, 2, 1, 3).reshape(depth, (CONV_W - 1) * db, 2 * d_ff)

    def n_major(a):
        return a.reshape(dec_seq, db, a.shape[-1]).transpose(1, 0, 2)

    def t_major(a):
        return a.transpose(1, 0, 2).reshape(m_s, a.shape[-1])

    def attn_b_sample(q, kv, bias):
        kv_new = n_major(kv).reshape(db, dec_seq * 2 * heads_b, HEAD_DIM)
        kv_pad = jnp.concatenate([kv_new, jnp.zeros((db, (page - dec_seq) * 2 * heads_b, HEAD_DIM), F32)], axis=1)
        return t_major(_sb_sample(n_major(q), kv_pad, pages, page_table, bias, heads=heads_b, page=page))

    bm_p = _pick(seq, (1024, 512, 256, 128))
    sb_blk = _pick(seq, (256, 128))
    cfg = dict(
        seq=seq, bm=bm_p, bm_norm=_pick(seq, (256, 128)), bm_down=_pick(seq, (2048, 1024, 512, 256, 128)),
        bn_down=1024, tap_s=db,
        tab_map=lambda j, i: (i % (seq // bm_p), 0),
        attn_a_p=lambda qkv: _attn_a_prompt(qkv, batch=batch, seq=seq, heads=heads_a),
        attn_b_p=lambda q, kv, bias: _sb_prompt(q, kv, bias, batch=batch, seq=seq, heads=heads_b, blk=sb_blk),
        attn_a_s=lambda qkv, l: t_major(_attn_a_sample(n_major(qkv), caches, l, heads=heads_a)),
        attn_b_s=attn_b_sample,
    )
    pos_s = past_len + jnp.repeat(jnp.arange(dec_seq), db)
    (y_p, qkv_p, kv_p, tails_p), (y_s, qkv_s, kv_s, tails_s) = _trunk(
        x_prompt.reshape(batch * seq, d), x_s,
        [prompt_mods(mod_all[l], 6) for l in range(depth)], [sample_mods(mod_all[l], 6) for l in range(depth)],
        prompt_mods(mod_kv_all, 2), sample_mods(mod_kv_all, 2),
        _rope_table(jnp.arange(seq)), _rope_table(pos_s), prev, cfg, weights)

    win_p = []
    for g in range(N_GROUPS_A):
        keep = min(WINDOWS[g], seq)
        per_layer = [q.reshape(batch, seq, N_GROUPS_A, 3, heads_a, HEAD_DIM)[:, seq - keep:, g, 1:3]
                     for q in qkv_p]
        win_p.append(jnp.stack(per_layer, axis=0))
    kv_prompt = kv_p.reshape(batch, seq, 2, heads_b, HEAD_DIM)
    nblk = seq // bm_p
    conv_p = jnp.stack([
        jnp.concatenate([tg.reshape(batch, nblk, SUBLANES, d_ff)[:, -1, SUBLANES - (CONV_W - 1):],
                         tu.reshape(batch, nblk, SUBLANES, d_ff)[:, -1, SUBLANES - (CONV_W - 1):]], axis=-1)
        for tg, tu in tails_p], axis=0)

    qkv_new = jnp.stack([n_major(q) for q in qkv_s], axis=0).reshape(
        n_a, db, dec_seq, N_GROUPS_A, 3, heads_a, HEAD_DIM)
    rh = 2 * heads_a
    shifted = _shift_caches(
        [c.reshape(n_a * db, c.shape[2] * rh, HEAD_DIM) for c in caches],
        [qkv_new[:, :, :, g, 1:3].reshape(n_a * db, dec_seq * rh, HEAD_DIM) for g in range(N_GROUPS_A)])
    win_s = [o.reshape(c.shape) for o, c in zip(shifted, caches)]
    kv_sample = n_major(kv_s).reshape(db, dec_seq, 2, heads_b, HEAD_DIM)
    conv_s = jnp.stack([
        jnp.concatenate([tg, tu], axis=-1).reshape(CONV_W - 1, db, 2 * d_ff).transpose(1, 0, 2)
        for tg, tu in tails_s], axis=0)

    y_prompt = y_p.reshape(batch, seq, d)
    y_sample = y_s.reshape(dec_seq, db, d).transpose(1, 0, 2)
    return (y_prompt, y_sample, win_p[0], win_s[0], win_p[1], win_s[1], win_p[2], win_s[2],
            kv_prompt, kv_sample, conv_p, conv_s)
```

```python
import functools

import jax
import jax.numpy as jnp
from jax import lax
from jax.experimental import pallas as pl
from jax.experimental.pallas import tpu as pltpu

F32 = jnp.float32
BF16 = jnp.bfloat16

HEAD_DIM = 128
N_GROUPS_A = 3
WINDOWS = (128, 512, 2048)
DILATIONS = (1, 4, 16)
ROT_DIM = HEAD_DIM // 4
ROPE_THETA = 500000.0
CONV_W = 3
RMS_EPS = 1e-6
NEG_BIG = -1e30
UNROLL_A = 4
K_BLOCK_BIG = 512
BAND = 128
SUBLANES = 8
ADA_ROWS = 16
VMEM_BIG = 56 * 1024 * 1024
VMEM_MID = 48 * 1024 * 1024
VMEM_DOWN = 44 * 1024 * 1024


def _params(sem, vmem=VMEM_MID):
    return pltpu.CompilerParams(dimension_semantics=sem, vmem_limit_bytes=vmem)


def _sigmoid(x):
    return 1.0 / (1.0 + jnp.exp(-x))


def _norm_kernel(x_ref, g_ref, sc_ref, sh_ref, o_ref):
    x = x_ref[...]
    ms = jnp.mean(x * x, axis=-1, keepdims=True)
    y = x * lax.rsqrt(ms + RMS_EPS) * g_ref[...]
    o_ref[...] = (y * (1.0 + sc_ref[...]) + sh_ref[...]).astype(o_ref.dtype)


def _final_norm_kernel(x_ref, g_ref, o_ref):
    x = x_ref[...]
    ms = jnp.mean(x * x, axis=-1, keepdims=True)
    o_ref[...] = (x * lax.rsqrt(ms + RMS_EPS) * g_ref[...]).astype(o_ref.dtype)


def _norm_mod(x, g, sc, sh, *, bm, rows_per_mod):
    m, d = x.shape
    r = sc.shape[1]
    mod_spec = pl.BlockSpec((None, r, d), lambda i: ((i * bm) // rows_per_mod, 0, 0))
    return pl.pallas_call(
        _norm_kernel,
        grid=(m // bm,),
        in_specs=[pl.BlockSpec((bm, d), lambda i: (i, 0)),
                  pl.BlockSpec((1, d), lambda i: (0, 0)),
                  mod_spec, mod_spec],
        out_specs=pl.BlockSpec((bm, d), lambda i: (i, 0)),
        out_shape=jax.ShapeDtypeStruct((m, d), BF16),
        compiler_params=_params(("arbitrary",)),
        name="norm_mod",
    )(x, g, sc, sh)


def _final_norm(x, g, *, bm):
    m, d = x.shape
    return pl.pallas_call(
        _final_norm_kernel,
        grid=(m // bm,),
        in_specs=[pl.BlockSpec((bm, d), lambda i: (i, 0)),
                  pl.BlockSpec((1, d), lambda i: (0, 0))],
        out_specs=pl.BlockSpec((bm, d), lambda i: (i, 0)),
        out_shape=jax.ShapeDtypeStruct((m, d), F32),
        compiler_params=_params(("arbitrary",)),
        name="final_norm",
    )(x, g)


def _cast_weight(w_ref, wb_ref):
    k = w_ref.shape[0]
    chunk = 512 if k % 512 == 0 else k

    def body(c, _):
        r0 = pl.multiple_of(c * chunk, chunk)
        wb_ref[pl.ds(r0, chunk), :] = w_ref[pl.ds(r0, chunk), :].astype(BF16)
        return 0

    lax.fori_loop(0, k // chunk, body, 0)


def _rope_tile(acc, tab_ref):
    cos = tab_ref[:, 0:HEAD_DIM]
    s_lo = tab_ref[:, HEAD_DIM:2 * HEAD_DIM]
    s_hi = tab_ref[:, 2 * HEAD_DIM:3 * HEAD_DIM]
    half = ROT_DIM // 2
    pieces = []
    for hh in range(acc.shape[1] // HEAD_DIM):
        x = acc[:, hh * HEAD_DIM:(hh + 1) * HEAD_DIM]
        nxt = pltpu.roll(x, HEAD_DIM - half, axis=1)
        prv = pltpu.roll(x, half, axis=1)
        pieces.append(x * cos + nxt * s_lo + prv * s_hi)
    return pieces


_N_EXTRA = {"plain": 0, "bias": 1, "resid": 2, "rope": 1}


def _mm_epilogue(acc, extra, o_ref, *, mode, qkv_width, out_scale):
    if mode == "plain":
        if out_scale != 1.0:
            acc = acc * out_scale
        o_ref[...] = acc.astype(o_ref.dtype)
    elif mode == "bias":
        o_ref[...] = (acc + extra[0][...]).astype(o_ref.dtype)
    elif mode == "resid":
        res_ref, gate_ref = extra
        o_ref[...] = (res_ref[...] + gate_ref[...] * acc).astype(o_ref.dtype)
    else:
        bn = acc.shape[1]
        kind = ((pl.program_id(0) * bn) // qkv_width) % 3

        @pl.when(kind == 2)
        def _():
            o_ref[...] = acc.astype(o_ref.dtype)

        @pl.when(kind != 2)
        def _():
            for hh, piece in enumerate(_rope_tile(acc, extra[0])):
                o_ref[:, hh * HEAD_DIM:(hh + 1) * HEAD_DIM] = piece.astype(o_ref.dtype)


def _mm_kernel(*refs, mode, silu_in, qkv_width, out_scale, side):
    ne = _N_EXTRA[mode]
    x_ref, w_ref = refs[0], refs[1]
    extra = refs[2:2 + ne]
    pos = 2 + ne
    if side:
        xs_ref, extra_s = refs[pos], refs[pos + 1:pos + 1 + ne]
        pos += 1 + ne
    o_ref = refs[pos]
    os_ref = refs[pos + 1] if side else None
    wb_ref = refs[-1]
    epilogue = functools.partial(_mm_epilogue, mode=mode, qkv_width=qkv_width, out_scale=out_scale)

    @pl.when(pl.program_id(1) == 0)
    def _():
        _cast_weight(w_ref, wb_ref)
        if side:
            acc_s = jnp.dot(xs_ref[...].astype(BF16), wb_ref[...], preferred_element_type=F32)
            epilogue(acc_s, extra_s, os_ref)

    x = x_ref[...]
    if silu_in:
        x = x * _sigmoid(x)
    epilogue(jnp.dot(x.astype(BF16), wb_ref[...], preferred_element_type=F32), extra, o_ref)


def _matmul(x, w, layer, *, bm, bn, out_dtype, mode="plain", extras=(), extra_specs=(),
            silu_in=False, qkv_width=0, out_scale=1.0, side=None, vmem=VMEM_BIG, name="matmul"):
    m, k = x.shape
    n = w.shape[2]
    kern = functools.partial(_mm_kernel, mode=mode, silu_in=silu_in, qkv_width=qkv_width, out_scale=out_scale,
                             side=side is not None)
    in_specs = [pl.BlockSpec((bm, k), lambda j, i: (i, 0)),
                pl.BlockSpec((None, k, bn), lambda j, i: (layer, 0, j)),
                *extra_specs]
    args = [x, w, *extras]
    out_specs = pl.BlockSpec((bm, bn), lambda j, i: (i, j))
    out_shape = jax.ShapeDtypeStruct((m, n), out_dtype)
    if side is not None:
        xs, extras_s, specs_s, dtype_s = side
        ms = xs.shape[0]
        in_specs += [pl.BlockSpec((ms, k), lambda j, i: (0, 0)), *specs_s]
        args += [xs, *extras_s]
        out_specs = [out_specs, pl.BlockSpec((ms, bn), lambda j, i: (0, j))]
        out_shape = [out_shape, jax.ShapeDtypeStruct((ms, n), dtype_s)]
    return pl.pallas_call(
        kern,
        grid=(n // bn, m // bm),
        in_specs=in_specs,
        out_specs=out_specs,
        out_shape=out_shape,
        scratch_shapes=[pltpu.VMEM((k, bn), BF16)],
        compiler_params=_params(("arbitrary", "arbitrary"), vmem),
        name=name,
    )(*args)


def _mm_ktiled_kernel(*refs, nk_big, nk_small, side):
    nw = 2 if nk_small else 1
    per = nw + 2
    w_refs = refs[:nw]
    groups = [refs[nw:nw + per]] + ([refs[nw + per:nw + 2 * per]] if side else [])
    outs = refs[nw + len(groups) * per:]
    kk = pl.program_id(2)
    first_rows = pl.program_id(1) == 0

    def each_group(fn):
        fn(groups[0], outs[0])
        if side:
            @pl.when(first_rows)
            def _():
                fn(groups[1], outs[1])

    @pl.when(kk == 0)
    def _():
        wb = w_refs[0][...].astype(BF16)

        def start(g, o_ref):
            o_ref[...] = jnp.dot(g[0][...], wb, preferred_element_type=F32)
        each_group(start)

    @pl.when(jnp.logical_and(kk > 0, kk < nk_big))
    def _():
        wb = w_refs[0][...].astype(BF16)

        def add(g, o_ref):
            o_ref[...] += jnp.dot(g[0][...], wb, preferred_element_type=F32)
        each_group(add)

    if nk_small:
        @pl.when(kk >= nk_big)
        def _():
            ws = w_refs[1][...].astype(BF16)

            def add_small(g, o_ref):
                o_ref[...] += jnp.dot(g[1][...], ws, preferred_element_type=F32)
            each_group(add_small)

    @pl.when(kk == nk_big + nk_small - 1)
    def _():
        def finish(g, o_ref):
            o_ref[...] = g[nw][...] + g[nw + 1][...] * o_ref[...]
        each_group(finish)


def _matmul_ktiled_resid(x, w, layer, res, gate, *, bm, bn, rows_per_mod, side=None, name="down"):
    m, k = x.shape
    n = w.shape[2]
    r = gate.shape[1]
    bk_big = min(k, K_BLOCK_BIG)
    nk_big = k // bk_big
    rem = k - nk_big * bk_big
    bk_small = _pick(rem, (256, 128)) if rem else 0
    nk_small = rem // bk_small if rem else 0
    small0 = (nk_big * bk_big) // bk_small if rem else 0
    assert rem == 0 or (nk_big * bk_big) % bk_small == 0

    def big(kk):
        return jnp.minimum(kk, nk_big - 1)

    def small(kk):
        return small0 + jnp.maximum(kk - nk_big, 0)

    in_specs = [pl.BlockSpec((None, bk_big, bn), lambda j, i, kk: (layer, big(kk), j))]
    args = [w]
    if nk_small:
        in_specs.append(pl.BlockSpec((None, bk_small, bn), lambda j, i, kk: (layer, small(kk), j)))
        args.append(w)
    in_specs.append(pl.BlockSpec((bm, bk_big), lambda j, i, kk: (i, big(kk))))
    args.append(x)
    if nk_small:
        in_specs.append(pl.BlockSpec((bm, bk_small), lambda j, i, kk: (i, small(kk))))
        args.append(x)
    in_specs += [pl.BlockSpec((bm, bn), lambda j, i, kk: (i, j), pipeline_mode=pl.Buffered(1)),
                 pl.BlockSpec((None, r, bn), lambda j, i, kk: ((i * bm) // rows_per_mod, 0, j))]
    args += [res, gate]
    out_specs = pl.BlockSpec((bm, bn), lambda j, i, kk: (i, j))
    out_shape = jax.ShapeDtypeStruct((m, n), F32)
    if side is not None:
        xs, res_s, gate_s = side
        ms = xs.shape[0]
        in_specs.append(pl.BlockSpec((ms, bk_big), lambda j, i, kk: (0, big(kk))))
        args.append(xs)
        if nk_small:
            in_specs.append(pl.BlockSpec((ms, bk_small), lambda j, i, kk: (0, small(kk))))
            args.append(xs)
        in_specs += [pl.BlockSpec((ms, bn), lambda j, i, kk: (0, j)),
                     pl.BlockSpec((None, ms, bn), lambda j, i, kk: (0, 0, j))]
        args += [res_s, gate_s]
        out_specs = [out_specs, pl.BlockSpec((ms, bn), lambda j, i, kk: (0, j))]
        out_shape = [out_shape, jax.ShapeDtypeStruct((ms, n), F32)]
    return pl.pallas_call(
        functools.partial(_mm_ktiled_kernel, nk_big=nk_big, nk_small=nk_small, side=side is not None),
        grid=(n // bn, m // bm, nk_big + nk_small),
        in_specs=in_specs,
        out_specs=out_specs,
        out_shape=out_shape,
        compiler_params=_params(("arbitrary", "arbitrary", "arbitrary"), VMEM_DOWN),
        name=name,
    )(*args)


def _ffn_rows(h_ref, act_ref, stg_ref, stu_ref, ug, uu, wgb, wub, conv_refs, *, rows, halo, tap):
    cwg_ref, cwu_ref, cbg_ref, cbu_ref = conv_refs
    h = h_ref[...]
    ug[halo:halo + rows, :] = jnp.dot(h, wgb[...], preferred_element_type=F32)
    uu[halo:halo + rows, :] = jnp.dot(h, wub[...], preferred_element_type=F32)

    def conv(u, cw_ref, cb_ref):
        y = cb_ref[...] + cw_ref[0:1, :] * u[halo - 2 * tap:halo - 2 * tap + rows, :]
        y = y + cw_ref[1:2, :] * u[halo - tap:halo - tap + rows, :]
        return y + cw_ref[2:3, :] * u[halo:halo + rows, :]

    gate = conv(ug, cwg_ref, cbg_ref)
    up = conv(uu, cwu_ref, cbu_ref)
    act_ref[...] = (gate * _sigmoid(gate) * up).astype(act_ref.dtype)
    tail_g = ug[rows:rows + halo, :]
    tail_u = uu[rows:rows + halo, :]
    stg_ref[...] = tail_g
    stu_ref[...] = tail_u
    ug[0:halo, :] = tail_g
    uu[0:halo, :] = tail_u


def _ffn_up_kernel(*refs, bm, halo, blocks_per_seq, side):
    h_ref, wg_ref, wu_ref = refs[0:3]
    conv_refs = refs[3:7]
    pos = 7
    if side:
        hs_ref, pgs_ref, pus_ref = refs[pos:pos + 3]
        pos += 3
    act_ref, stg_ref, stu_ref = refs[pos:pos + 3]
    pos += 3
    if side:
        acts_ref, stgs_ref, stus_ref = refs[pos:pos + 3]
        pos += 3
    wgb, wub, ug, uu = refs[pos:pos + 4]
    i = pl.program_id(1)

    @pl.when(i == 0)
    def _():
        _cast_weight(wg_ref, wgb)
        _cast_weight(wu_ref, wub)
        if side:
            halo_s, tap_s = side
            ugs, uus = refs[pos + 4:pos + 6]
            ugs[0:halo_s, :] = pgs_ref[...]
            uus[0:halo_s, :] = pus_ref[...]
            _ffn_rows(hs_ref, acts_ref, stgs_ref, stus_ref, ugs, uus, wgb, wub, conv_refs,
                      rows=hs_ref.shape[0], halo=halo_s, tap=tap_s)

    @pl.when(i % blocks_per_seq == 0)
    def _():
        ug[0:halo, :] = jnp.zeros((halo, ug.shape[1]), F32)
        uu[0:halo, :] = jnp.zeros((halo, uu.shape[1]), F32)

    _ffn_rows(h_ref, act_ref, stg_ref, stu_ref, ug, uu, wgb, wub, conv_refs, rows=bm, halo=halo, tap=1)


def _ffn_up(h, w_up, w_conv, b_conv3, layer, *, bm, bn, rows_per_seq, side=None):
    m, d = h.shape
    d_ff = w_up.shape[2] // 2
    nj = d_ff // bn
    halo = SUBLANES
    in_specs = [pl.BlockSpec((bm, d), lambda j, i: (i, 0)),
                pl.BlockSpec((None, d, bn), lambda j, i: (layer, 0, j)),
                pl.BlockSpec((None, d, bn), lambda j, i: (layer, 0, j + nj)),
                pl.BlockSpec((None, CONV_W, bn), lambda j, i: (layer, 0, j)),
                pl.BlockSpec((None, CONV_W, bn), lambda j, i: (layer, 0, j + nj)),
                pl.BlockSpec((None, 1, bn), lambda j, i: (layer, 0, j)),
                pl.BlockSpec((None, 1, bn), lambda j, i: (layer, 0, j + nj))]
    args = [h, w_up, w_up, w_conv, w_conv, b_conv3, b_conv3]
    st_spec = pl.BlockSpec((None, halo, bn), lambda j, i: (i, 0, j))
    out_specs = [pl.BlockSpec((bm, bn), lambda j, i: (i, j)), st_spec, st_spec]
    out_shape = [jax.ShapeDtypeStruct((m, d_ff), BF16),
                 jax.ShapeDtypeStruct((m // bm, halo, d_ff), F32),
                 jax.ShapeDtypeStruct((m // bm, halo, d_ff), F32)]
    scratch = [pltpu.VMEM((d, bn), BF16), pltpu.VMEM((d, bn), BF16),
               pltpu.VMEM((bm + halo, bn), F32), pltpu.VMEM((bm + halo, bn), F32)]
    side_cfg = None
    if side is not None:
        hs, prev, tap_s = side
        ms = hs.shape[0]
        halo_s = (CONV_W - 1) * tap_s
        side_cfg = (halo_s, tap_s)
        in_specs += [pl.BlockSpec((ms, d), lambda j, i: (0, 0)),
                     pl.BlockSpec((halo_s, bn), lambda j, i: (0, j)),
                     pl.BlockSpec((halo_s, bn), lambda j, i: (0, j + nj))]
        args += [hs, prev, prev]
        sts_spec = pl.BlockSpec((halo_s, bn), lambda j, i: (0, j))
        out_specs += [pl.BlockSpec((ms, bn), lambda j, i: (0, j)), sts_spec, sts_spec]
        out_shape += [jax.ShapeDtypeStruct((ms, d_ff), BF16),
                      jax.ShapeDtypeStruct((halo_s, d_ff), F32), jax.ShapeDtypeStruct((halo_s, d_ff), F32)]
        scratch += [pltpu.VMEM((ms + halo_s, bn), F32), pltpu.VMEM((ms + halo_s, bn), F32)]
    kern = functools.partial(_ffn_up_kernel, bm=bm, halo=halo, blocks_per_seq=rows_per_seq // bm, side=side_cfg)
    return pl.pallas_call(
        kern,
        grid=(nj, m // bm),
        in_specs=in_specs,
        out_specs=out_specs,
        out_shape=out_shape,
        scratch_shapes=scratch,
        compiler_params=_params(("arbitrary", "arbitrary"), VMEM_BIG),
        name="ffn_up",
    )(*args)


def _attn_a_prompt_kernel(q0, k0, v0, q1, k1, v1, q2, k2, v2, o_ref, og, lg, *, seq):
    qkv = ((q0, k0, v0), (q1, k1, v1), (q2, k2, v2))
    scale = HEAD_DIM ** -0.5
    row = lax.broadcasted_iota(jnp.int32, (BAND, 2 * BAND), 0)
    col = lax.broadcasted_iota(jnp.int32, (BAND, 2 * BAND), 1)
    cur_ok = jnp.logical_and(col >= BAND, col - BAND <= row)
    prev_ok = jnp.logical_and(col < BAND, col >= row)
    dn = (((1,), (1,)), ((), ()))

    for g in range(N_GROUPS_A):
        dil = DILATIONS[g]
        nb = seq // dil // BAND
        q_ref, k_ref, v_ref = qkv[g]
        unroll = UNROLL_A if (dil * nb) % UNROLL_A == 0 else 1

        def rows(start, dil=dil):
            if dil == 1:
                return pl.ds(pl.multiple_of(start, BAND), BAND)
            return pl.ds(start, BAND, stride=dil)

        def body(it, _, g=g, dil=dil, nb=nb, q_ref=q_ref, k_ref=k_ref, v_ref=v_ref, rows=rows, unroll=unroll):
            starts, oks, scores, vbands = [], [], [], []
            for u in range(unroll):
                idx = it * unroll + u
                blk = idx % nb
                start = idx // nb + blk * (BAND * dil)
                has_prev = blk > 0
                pstart = jnp.where(has_prev, start - BAND * dil, start)
                q = q_ref[rows(start), :].astype(BF16)
                kband = jnp.concatenate([k_ref[rows(pstart), :], k_ref[rows(start), :]], axis=0).astype(BF16)
                vbands.append(jnp.concatenate([v_ref[rows(pstart), :], v_ref[rows(start), :]], axis=0).astype(BF16))
                scores.append(lax.dot_general(q, kband, dn, preferred_element_type=F32))
                starts.append(start)
                oks.append(jnp.logical_or(cur_ok, jnp.logical_and(prev_ok, has_prev)))
            probs, dens, lses = [], [], []
            for u in range(unroll):
                sc = jnp.where(oks[u], scores[u] * scale, NEG_BIG)
                mx = jnp.max(sc, axis=1, keepdims=True)
                p = jnp.exp(sc - mx)
                den = jnp.sum(p, axis=1, keepdims=True)
                probs.append(p.astype(BF16))
                dens.append(den)
                lses.append(mx + jnp.log(den))
            for u in range(unroll):
                o = jnp.dot(probs[u], vbands[u], preferred_element_type=F32)
                og[g, rows(starts[u]), :] = o / dens[u]
                lg[g, rows(starts[u]), :] = jnp.broadcast_to(lses[u], (BAND, HEAD_DIM))
            return 0

        lax.fori_loop(0, dil * nb // unroll, body, 0)

    chunk = 256

    def comb(c, _):
        r0 = pl.multiple_of(c * chunk, chunk)
        l0 = lg[0, pl.ds(r0, chunk), :]
        l1 = lg[1, pl.ds(r0, chunk), :]
        l2 = lg[2, pl.ds(r0, chunk), :]
        mx = jnp.maximum(jnp.maximum(l0, l1), l2)
        e0 = jnp.exp(l0 - mx)
        e1 = jnp.exp(l1 - mx)
        e2 = jnp.exp(l2 - mx)
        tot = e0 + e1 + e2
        o = (og[0, pl.ds(r0, chunk), :] * (e0 / tot) + og[1, pl.ds(r0, chunk), :] * (e1 / tot)
             + og[2, pl.ds(r0, chunk), :] * (e2 / tot))
        o_ref[pl.ds(r0, chunk), :] = o.astype(o_ref.dtype)
        return 0

    lax.fori_loop(0, seq // chunk, comb, 0)


def _attn_a_prompt(qkv, *, batch, seq, heads):
    m = qkv.shape[0]
    specs = []
    for g in range(N_GROUPS_A):
        for t in range(3):
            specs.append(pl.BlockSpec((seq, HEAD_DIM),
                                      lambda n, h, g=g, t=t: (n, (g * 3 + t) * heads + h)))
    return pl.pallas_call(
        functools.partial(_attn_a_prompt_kernel, seq=seq),
        grid=(batch, heads),
        in_specs=specs,
        out_specs=pl.BlockSpec((seq, HEAD_DIM), lambda n, h: (n, h)),
        out_shape=jax.ShapeDtypeStruct((m, heads * HEAD_DIM), BF16),
        scratch_shapes=[pltpu.VMEM((N_GROUPS_A, seq, HEAD_DIM), F32),
                        pltpu.VMEM((N_GROUPS_A, seq, HEAD_DIM), F32)],
        compiler_params=_params(("arbitrary", "arbitrary")),
        name="attn_a_prompt",
    )(*([qkv] * 9))


def _attn_a_sample_kernel(qn_ref, c0_ref, c1_ref, c2_ref, o_ref, *, heads, dec_seq):
    width = heads * HEAD_DIM
    scale = HEAD_DIM ** -0.5
    caches = (c0_ref, c1_ref, c2_ref)
    hrow = lax.broadcasted_iota(jnp.int32, (heads, width), 0)
    hcol = lax.broadcasted_iota(jnp.int32, (heads, width), 1) // HEAD_DIM
    head_mask = hrow == hcol
    key_idx = lax.broadcasted_iota(jnp.int32, (heads, BAND), 1)
    dn = (((1,), (1,)), ((), ()))

    def split_heads(x):
        y = pltpu.einshape("mrd->rmd", x)
        k = jnp.concatenate([y[h] for h in range(heads)], axis=1).astype(BF16)
        v = jnp.concatenate([y[heads + h] for h in range(heads)], axis=1).astype(BF16)
        return k, v

    rh = 2 * heads
    kv_bufs = [[split_heads(caches[0][...])]]
    for g in range(1, N_GROUPS_A):
        kv_bufs.append([split_heads(caches[g][:, t * rh:(t + 1) * rh, :]) for t in range(dec_seq)])

    for t in range(dec_seq):
        outs, lses = [], []
        for g in range(N_GROUPS_A):
            base = g * 3 * width
            q_row = qn_ref[t:t + 1, base:base + width]
            qbd = jnp.where(head_mask, jnp.broadcast_to(q_row, (heads, width)), 0.0)
            kb, vb = kv_bufs[g][0 if g == 0 else t]
            s_buf = lax.dot_general(qbd.astype(BF16), kb, dn, preferred_element_type=F32) * scale
            if g == 0:
                s_buf = jnp.where(key_idx >= t, s_buf, NEG_BIG)
                new_rows = range(t + 1)
            else:
                new_rows = (t,)
            s_new = []
            for tn in new_rows:
                k_row = qn_ref[tn:tn + 1, base + width:base + 2 * width]
                s_new.append(jnp.sum(qbd * k_row, axis=1, keepdims=True) * scale)
            mx = jnp.max(s_buf, axis=1, keepdims=True)
            for s in s_new:
                mx = jnp.maximum(mx, s)
            p_buf = jnp.exp(s_buf - mx)
            den = jnp.sum(p_buf, axis=1, keepdims=True)
            o = jnp.dot(p_buf.astype(BF16), vb, preferred_element_type=F32)
            for tn, s in zip(new_rows, s_new):
                p = jnp.exp(s - mx)
                den = den + p
                o = o + p * qn_ref[tn:tn + 1, base + 2 * width:base + 3 * width]
            o = jnp.where(head_mask, o / den, 0.0)
            lse = jnp.where(head_mask, jnp.broadcast_to(mx + jnp.log(den), (heads, width)), 0.0)
            outs.append(jnp.sum(o, axis=0, keepdims=True))
            lses.append(jnp.sum(lse, axis=0, keepdims=True))
        mx = jnp.maximum(jnp.maximum(lses[0], lses[1]), lses[2])
        es = [jnp.exp(l - mx) for l in lses]
        tot = es[0] + es[1] + es[2]
        row = outs[0] * (es[0] / tot) + outs[1] * (es[1] / tot) + outs[2] * (es[2] / tot)
        o_ref[t:t + 1, :] = row.astype(o_ref.dtype)


def _attn_a_sample(qn, caches, layer, *, heads):
    db, dec_seq, _ = qn.shape
    width = heads * HEAD_DIM
    views, specs = [], []
    rh = 2 * heads
    for g in range(N_GROUPS_A):
        c = caches[g]
        nlay, _, nbuf = c.shape[:3]
        dil = DILATIONS[g]
        views.append(c.reshape(nlay, db, nbuf // dil, dil * rh, HEAD_DIM))
        need = rh if g == 0 else dec_seq * rh
        blk_r = need if need == dil * rh else -(-need // SUBLANES) * SUBLANES
        specs.append(pl.BlockSpec((None, None, nbuf // dil, blk_r, HEAD_DIM), lambda n: (layer, n, 0, 0, 0)))
    return pl.pallas_call(
        functools.partial(_attn_a_sample_kernel, heads=heads, dec_seq=dec_seq),
        grid=(db,),
        in_specs=[pl.BlockSpec((None, dec_seq, qn.shape[2]), lambda n: (n, 0, 0)), *specs],
        out_specs=pl.BlockSpec((None, dec_seq, width), lambda n: (n, 0, 0)),
        out_shape=jax.ShapeDtypeStruct((db, dec_seq, width), F32),
        compiler_params=_params(("arbitrary",)),
        name="attn_a_sample",
    )(qn, *views)


LOG2E = 1.4426950408889634
LN2 = 0.6931471805599453
SB_QSCALE = HEAD_DIM ** -0.5 * LOG2E


def _sb_blocks(x2s, tri, masks, vs):
    sps = []
    for x2, mask in zip(x2s, masks):
        neg_abs = pltpu.bitcast(pltpu.bitcast(x2, jnp.uint32) | jnp.uint32(0x80000000), F32)
        sp = LN2 * jnp.maximum(x2, 0.0) + jnp.log(1.0 + jnp.exp2(neg_abs))
        sps.append(sp if mask is None else jnp.where(mask, sp, 0.0))
    incls = [jnp.dot(sp.astype(BF16), tri, preferred_element_type=F32) for sp in sps]
    ws = []
    for x2, incl, mask in zip(x2s, incls, masks):
        a = jnp.exp2(x2 - LOG2E * incl)
        ws.append((a if mask is None else jnp.where(mask, a, 0.0)).astype(BF16))
    pvs = [jnp.dot(a, v, preferred_element_type=F32) for a, v in zip(ws, vs)]
    return [(pv, jnp.sum(sp, axis=1, keepdims=True)) for pv, sp in zip(pvs, sps)]


def _sb_prompt_kernel(q_ref, k_ref, v_ref, b_ref, tri_ref, o_ref, kb_ref, vb_ref, *, seq, blk, hpb):
    nblk = seq // blk
    chunk = 512 if seq % 512 == 0 else seq

    def cast(c, _):
        r0 = pl.multiple_of(c * chunk, chunk)
        kb_ref[pl.ds(r0, chunk), :] = k_ref[pl.ds(r0, chunk), :].astype(BF16)
        vb_ref[pl.ds(r0, chunk), :] = v_ref[pl.ds(r0, chunk), :].astype(BF16)
        return 0

    lax.fori_loop(0, seq // chunk, cast, 0)
    row = lax.broadcasted_iota(jnp.int32, (blk, blk), 0)
    col = lax.broadcasted_iota(jnp.int32, (blk, blk), 1)
    diag_mask = col < row
    dn = (((1,), (1,)), ((), ()))
    lanes = [slice(hh * HEAD_DIM, (hh + 1) * HEAD_DIM) for hh in range(hpb)]

    def q_block(qb, _):
        q0 = pl.multiple_of(qb * blk, blk)
        qs = [q_ref[pl.ds(q0, blk), ln] for ln in lanes]

        def blocks(kbs, mask):
            k0s = [pl.multiple_of(kb * blk, blk) for kb in kbs]
            x2s = [lax.dot_general(qs[hh], kb_ref[pl.ds(k0, blk), lanes[hh]], dn,
                                   preferred_element_type=F32) + b_ref[hh]
                   for hh in range(hpb) for k0 in k0s]
            vs = [vb_ref[pl.ds(k0, blk), lanes[hh]] for hh in range(hpb) for k0 in k0s]
            out = _sb_blocks(x2s, tri_ref[...], [mask] * len(x2s), vs)
            return [out[hh * len(kbs):(hh + 1) * len(kbs)] for hh in range(hpb)]

        state = tuple(res[0] for res in blocks([qb], diag_mask))

        def pair(it, state):
            res = blocks([qb - 1 - 2 * it, qb - 2 - 2 * it], None)
            out = []
            for (acc, carry), ((pv0, m0), (pv1, m1)) in zip(state, res):
                acc = acc + jnp.exp(-carry) * pv0 + jnp.exp(-(carry + m0)) * pv1
                out.append((acc, carry + m0 + m1))
            return tuple(out)

        state = lax.fori_loop(0, qb // 2, pair, state)

        def last(state):
            res = blocks([0], None)
            return tuple((acc + jnp.exp(-carry) * pv, carry + m)
                         for (acc, carry), ((pv, m),) in zip(state, res))

        state = lax.cond(qb % 2 == 1, last, lambda st: st, state)
        for hh, (acc, _) in enumerate(state):
            o_ref[pl.ds(q0, blk), lanes[hh]] = acc.astype(o_ref.dtype)
        return 0

    lax.fori_loop(0, nblk, q_block, 0)


def _tri(blk):
    r = lax.broadcasted_iota(jnp.int32, (blk, blk), 0)
    c = lax.broadcasted_iota(jnp.int32, (blk, blk), 1)
    return (r >= c).astype(BF16)


def _sb_prompt(q, kv, bias, *, batch, seq, heads, blk, hpb=4):
    m = q.shape[0]
    hpb = hpb if heads % hpb == 0 else 1
    w = hpb * HEAD_DIM
    bias_b = jnp.broadcast_to(LOG2E * bias.astype(F32)[:, None, None], (heads, 1, blk))
    return pl.pallas_call(
        functools.partial(_sb_prompt_kernel, seq=seq, blk=blk, hpb=hpb),
        grid=(batch, heads // hpb),
        in_specs=[pl.BlockSpec((seq, w), lambda n, h: (n, h)),
                  pl.BlockSpec((seq, w), lambda n, h: (n, h)),
                  pl.BlockSpec((seq, w), lambda n, h: (n, heads // hpb + h)),
                  pl.BlockSpec((hpb, 1, blk), lambda n, h: (h, 0, 0)),
                  pl.BlockSpec((blk, blk), lambda n, h: (0, 0))],
        out_specs=pl.BlockSpec((seq, w), lambda n, h: (n, h)),
        out_shape=jax.ShapeDtypeStruct((m, heads * HEAD_DIM), BF16),
        scratch_shapes=[pltpu.VMEM((seq, w), BF16), pltpu.VMEM((seq, w), BF16)],
        compiler_params=_params(("arbitrary", "arbitrary")),
        name="sb_prompt",
    )(q, kv, kv, bias_b, _tri(blk))


QROWS = SUBLANES


def _sb_sample_kernel(pt_ref, q_ref, new_ref, page_ref, b_ref, tri_ref, o_ref,
                      q3_ref, acc_ref, carry_ref, *, heads, dec_seq, page):
    del pt_ref
    p = pl.program_id(1)
    rows = heads * QROWS

    def process(kv_ref, mask):
        x = kv_ref[...].astype(BF16).reshape(page, 2 * heads, HEAD_DIM)
        k = pltpu.einshape("khd->hkd", x[:, 0:heads, :])
        v = pltpu.einshape("khd->hkd", x[:, heads:2 * heads, :])
        s = jnp.einsum("htd,hkd->htk", q3_ref[...].astype(BF16), k, preferred_element_type=F32)
        x2 = s.reshape(rows, page) + b_ref[...]
        neg_abs = pltpu.bitcast(pltpu.bitcast(x2, jnp.uint32) | jnp.uint32(0x80000000), F32)
        sp = LN2 * jnp.maximum(x2, 0.0) + jnp.log(1.0 + jnp.exp2(neg_abs))
        if mask is not None:
            sp = jnp.where(mask, sp, 0.0)
        incl = jnp.dot(sp.astype(BF16), tri_ref[...], preferred_element_type=F32)
        a = jnp.exp2(x2 - LOG2E * incl)
        if mask is not None:
            a = jnp.where(mask, a, 0.0)
        pv = jnp.einsum("htk,hkd->htd", a.astype(BF16).reshape(heads, QROWS, page), v,
                        preferred_element_type=F32).reshape(rows, HEAD_DIM)
        carry = carry_ref[:, 0:1]
        acc_ref[...] += jnp.exp(-carry) * pv
        carry_ref[...] = jnp.broadcast_to(carry + jnp.sum(sp, axis=1, keepdims=True), carry_ref.shape)

    @pl.when(p == 0)
    def _():
        q3_ref[...] = jnp.zeros_like(q3_ref)
        for h in range(heads):
            q3_ref[h, 0:dec_seq, :] = q_ref[:, h * HEAD_DIM:(h + 1) * HEAD_DIM]
        acc_ref[...] = jnp.zeros_like(acc_ref)
        carry_ref[...] = jnp.zeros_like(carry_ref)
        t_of_row = lax.broadcasted_iota(jnp.int32, (rows, page), 0) % QROWS
        key = lax.broadcasted_iota(jnp.int32, (rows, page), 1)
        process(new_ref, key < t_of_row)

    @pl.when(p > 0)
    def _():
        process(page_ref, None)

    @pl.when(p == pl.num_programs(1) - 1)
    def _():
        for h in range(heads):
            o_ref[:, h * HEAD_DIM:(h + 1) * HEAD_DIM] = acc_ref[h * QROWS:h * QROWS + dec_seq, :]


def _sb_sample(q, kv_new_pad, pages, page_table, bias, *, heads, page):
    db, dec_seq, width = q.shape
    assert dec_seq <= QROWS
    n_pages = page_table.shape[1]
    rows = heads * QROWS
    prow = page * 2 * heads
    bias_b = jnp.broadcast_to(LOG2E * jnp.repeat(bias.astype(F32), QROWS)[:, None], (rows, page))

    def page_map(n, p, pt):
        return (pt[n, n_pages - jnp.maximum(p, 1)], 0, 0)

    grid_spec = pltpu.PrefetchScalarGridSpec(
        num_scalar_prefetch=1,
        grid=(db, n_pages + 1),
        in_specs=[pl.BlockSpec((None, dec_seq, width), lambda n, p, pt: (n, 0, 0)),
                  pl.BlockSpec((None, prow, HEAD_DIM), lambda n, p, pt: (n, 0, 0)),
                  pl.BlockSpec((None, prow, HEAD_DIM), page_map),
                  pl.BlockSpec((rows, page), lambda n, p, pt: (0, 0)),
                  pl.BlockSpec((page, page), lambda n, p, pt: (0, 0))],
        out_specs=pl.BlockSpec((None, dec_seq, width), lambda n, p, pt: (n, 0, 0)),
        scratch_shapes=[pltpu.VMEM((heads, QROWS, HEAD_DIM), F32), pltpu.VMEM((rows, HEAD_DIM), F32),
                        pltpu.VMEM((rows, HEAD_DIM), F32)],
    )
    return pl.pallas_call(
        functools.partial(_sb_sample_kernel, heads=heads, dec_seq=dec_seq, page=page),
        grid_spec=grid_spec,
        out_shape=jax.ShapeDtypeStruct((db, dec_seq, width), F32),
        compiler_params=_params(("arbitrary", "arbitrary")),
        name="sb_sample",
    )(page_table, q, kv_new_pad, pages, bias_b, _tri(page))


def _rope_table(pos):
    half = ROT_DIM // 2
    inv = ROPE_THETA ** (-jnp.arange(half, dtype=F32) / half)
    ang = pos.astype(F32)[:, None] * inv[None, :]
    cos, sin = jnp.cos(ang), jnp.sin(ang)
    t = pos.shape[0]
    ones = jnp.ones((t, HEAD_DIM - ROT_DIM), F32)
    zeros = jnp.zeros((t, HEAD_DIM - ROT_DIM), F32)
    zh = jnp.zeros((t, half), F32)
    return jnp.concatenate([cos, cos, ones, -sin, zh, zeros, zh, sin, zeros], axis=1)


def _pick(v, cands):
    for c in cands:
        if v % c == 0:
            return c
    return v


def _trunk(xp, xs, mods_p, mods_s, modkv_p, modkv_s, tabs_p, tabs_s, prev, cfg, weights):
    (norm_attn_g, norm_ffn_g, w_qkv_a, w_o_a, w_q_b, w_o_b, sb_bias, norm_kv_g, w_kv,
     w_up, w_conv, b_conv3, w_down, norm_final_g) = weights
    d = xp.shape[1]
    ms = xs.shape[0]
    bm, bmn, seq = cfg["bm"], cfg["bm_norm"], cfg["seq"]
    depth = w_up.shape[0]
    n_a = w_qkv_a.shape[0]
    heads_a = w_o_a.shape[1] // HEAD_DIM
    d_ff = w_down.shape[1]

    def norm_both(g, sc_p, sh_p, sc_s, sh_s):
        return (_norm_mod(xp, g, sc_p, sh_p, bm=bmn, rows_per_mod=seq),
                _norm_mod(xs, g, sc_s, sh_s, bm=ms, rows_per_mod=ms))

    def tile(n, cands=(512, 256, 128)):
        return _pick(n, cands)

    def side_rows(bn):
        return (pl.BlockSpec((ms, bn), lambda j, i: (0, j)), pl.BlockSpec((None, ms, bn), lambda j, i: (0, 0, j)))

    qkv_p, qkv_s, tails_p, tails_s = [], [], [], []
    kv_p = kv_s = None
    for l in range(depth):
        sh1, sc1, g1, sh2, sc2, g2 = mods_p[l]
        sh1s, sc1s, g1s, sh2s, sc2s, g2s = mods_s[l]
        hp, hs = norm_both(norm_attn_g[l][None], sc1, sh1, sc1s, sh1s)
        if l < n_a:
            bn = tile(heads_a * HEAD_DIM)
            qp, qs = _matmul(hp, w_qkv_a, l, bm=bm, bn=bn, out_dtype=F32, mode="rope",
                             extras=(tabs_p,), extra_specs=(pl.BlockSpec((bm, 3 * HEAD_DIM), cfg["tab_map"]),),
                             side=(hs, (tabs_s,), (pl.BlockSpec((ms, 3 * HEAD_DIM), lambda j, i: (0, 0)),), F32),
                             qkv_width=heads_a * HEAD_DIM, name="qkv")
            qkv_p.append(qp)
            qkv_s.append(qs)
            op, os_ = cfg["attn_a_p"](qp), cfg["attn_a_s"](qs, l)
            w_o, lo = w_o_a, l
        else:
            j = l - n_a
            qp, qs = _matmul(hp, w_q_b, j, bm=bm, bn=tile(w_q_b.shape[2]), out_dtype=BF16,
                             out_scale=SB_QSCALE, side=(hs, (), (), F32), name="q_b")
            op, os_ = cfg["attn_b_p"](qp, kv_p, sb_bias[j]), cfg["attn_b_s"](qs, kv_s, sb_bias[j])
            w_o, lo = w_o_b, j
        bn = tile(d)
        xp, xs = _matmul(op, w_o, lo, bm=bm, bn=bn, out_dtype=F32, mode="resid", extras=(xp, g1),
                         extra_specs=(pl.BlockSpec((bm, bn), lambda j, i: (i, j)),
                                      pl.BlockSpec((None, 1, bn), lambda j, i: ((i * bm) // seq, 0, j))),
                         side=(os_, (xs, g1s), side_rows(bn), F32), name="attn_out")
        hp, hs = norm_both(norm_ffn_g[l][None], sc2, sh2, sc2s, sh2s)
        act_p, tg, tu, act_s, tgs, tus = _ffn_up(hp, w_up, w_conv, b_conv3, l, bm=bm, bn=_pick(d_ff, (256, 128)),
                                                 rows_per_seq=seq, side=(hs, prev[l], cfg["tap_s"]))
        tails_p.append((tg, tu))
        tails_s.append((tgs, tus))
        xp, xs = _matmul_ktiled_resid(act_p, w_down, l, xp, g2, bm=cfg["bm_down"],
                                      bn=_pick(d, (cfg["bn_down"], 512, 256, 128)), rows_per_mod=seq,
                                      side=(act_s, xs, g2s))
        if l == n_a - 1:
            hp, hs = norm_both(norm_kv_g[None], modkv_p[1], modkv_p[0], modkv_s[1], modkv_s[0])
            kv_p, kv_s = _matmul(hp, w_kv[None], 0, bm=bm, bn=tile(w_kv.shape[1]), out_dtype=F32,
                                 side=(hs, (), (), F32), name="kv")
    y_p = _final_norm(xp, norm_final_g[None], bm=bmn)
    y_s = _final_norm(xs, norm_final_g[None], bm=ms)
    return (y_p, qkv_p, kv_p, tails_p), (y_s, qkv_s, kv_s, tails_s)


def kernel(x_prompt, x_sample, cache_win0, cache_win1, cache_win2, cache_kv_pages, state_conv, page_table,
           c_prompt, c_sample, norm_attn_g, norm_ffn_g, w_ada, b_ada, w_qkv_a, w_o_a, w_q_b, w_o_b, sb_bias,
           norm_kv_g, w_ada_kv, b_ada_kv, w_kv, w_up, w_conv, b_conv, w_down, norm_final_g):
    batch, seq, d = x_prompt.shape
    db, dec_seq, _ = x_sample.shape
    depth = w_up.shape[0]
    n_a = w_qkv_a.shape[0]
    heads_a = w_o_a.shape[1] // HEAD_DIM
    heads_b = w_q_b.shape[2] // HEAD_DIM
    d_ff = w_down.shape[1]
    page = cache_kv_pages.shape[1]
    past_len = page_table.shape[1] * page
    caches = (cache_win0, cache_win1, cache_win2)
    assert seq % (BAND * DILATIONS[-1]) == 0 and batch + db <= ADA_ROWS and db == SUBLANES
    assert all(caches[g].shape[2] == WINDOWS[g] for g in range(N_GROUPS_A)) and dec_seq <= DILATIONS[1]

    c_all = jnp.concatenate([c_prompt, c_sample, jnp.zeros((ADA_ROWS - batch - db, d), F32)], axis=0)

    def ada(w, b, layer):
        n = w.shape[2]
        bn = _pick(n, (1024, 512, 256, 128))
        return _matmul(c_all, w, layer, bm=ADA_ROWS, bn=bn, out_dtype=F32, mode="bias", silu_in=True,
                       extras=(b.reshape(w.shape[0], 1, n),),
                       extra_specs=(pl.BlockSpec((None, 1, bn), lambda j, i: (layer, 0, j)),), name="ada")

    mod_all = [ada(w_ada, b_ada, l) for l in range(depth)]
    mod_kv_all = ada(w_ada_kv[None], b_ada_kv[None], 0)

    def prompt_mods(a, chunks):
        return tuple(c.reshape(batch, 1, d) for c in jnp.split(a[:batch], chunks, axis=-1))

    def sample_mods(a, chunks):
        return tuple(jnp.tile(c, (dec_seq, 1))[None] for c in jnp.split(a[batch:batch + db], chunks, axis=-1))

    b_conv3 = b_conv.reshape(depth, 1, 2 * d_ff)
    weights = (norm_attn_g, norm_ffn_g, w_qkv_a, w_o_a, w_q_b, w_o_b, sb_bias, norm_kv_g, w_kv,
               w_up, w_conv, b_conv3, w_down, norm_final_g)

    m_s = dec_seq * db
    x_s = x_sample.transpose(1, 0, 2).reshape(m_s, d)
    pages = cache_kv_pages.reshape(cache_kv_pages.shape[0], page * 2 * heads_b, HEAD_DIM)
    prev = state_conv.transpose(0, 2, 1, 3).reshape(depth, (CONV_W - 1) * db, 2 * d_ff)

    def n_major(a):
        return a.reshape(dec_seq, db, a.shape[-1]).transpose(1, 0, 2)

    def t_major(a):
        return a.transpose(1, 0, 2).reshape(m_s, a.shape[-1])

    def attn_b_sample(q, kv, bias):
        kv_new = n_major(kv).reshape(db, dec_seq * 2 * heads_b, HEAD_DIM)
        kv_pad = jnp.concatenate([kv_new, jnp.zeros((db, (page - dec_seq) * 2 * heads_b, HEAD_DIM), F32)], axis=1)
        return t_major(_sb_sample(n_major(q), kv_pad, pages, page_table, bias, heads=heads_b, page=page))

    bm_p = _pick(seq, (1024, 512, 256, 128))
    sb_blk = _pick(seq, (256, 128))
    cfg = dict(
        seq=seq, bm=bm_p, bm_norm=_pick(seq, (256, 128)), bm_down=_pick(seq, (2048, 1024, 512, 256, 128)),
        bn_down=1024, tap_s=db,
        tab_map=lambda j, i: (i % (seq // bm_p), 0),
        attn_a_p=lambda qkv: _attn_a_prompt(qkv, batch=batch, seq=seq, heads=heads_a),
        attn_b_p=lambda q, kv, bias: _sb_prompt(q, kv, bias, batch=batch, seq=seq, heads=heads_b, blk=sb_blk),
        attn_a_s=lambda qkv, l: t_major(_attn_a_sample(n_major(qkv), caches, l, heads=heads_a)),
        attn_b_s=attn_b_sample,
    )
    pos_s = past_len + jnp.repeat(jnp.arange(dec_seq), db)
    (y_p, qkv_p, kv_p, tails_p), (y_s, qkv_s, kv_s, tails_s) = _trunk(
        x_prompt.reshape(batch * seq, d), x_s,
        [prompt_mods(mod_all[l], 6) for l in range(depth)], [sample_mods(mod_all[l], 6) for l in range(depth)],
        prompt_mods(mod_kv_all, 2), sample_mods(mod_kv_all, 2),
        _rope_table(jnp.arange(seq)), _rope_table(pos_s), prev, cfg, weights)

    win_p = []
    for g in range(N_GROUPS_A):
        keep = min(WINDOWS[g], seq)
        per_layer = [q.reshape(batch, seq, N_GROUPS_A, 3, heads_a, HEAD_DIM)[:, seq - keep:, g, 1:3]
                     for q in qkv_p]
        win_p.append(jnp.stack(per_layer, axis=0))
    kv_prompt = kv_p.reshape(batch, seq, 2, heads_b, HEAD_DIM)
    nblk = seq // bm_p
    conv_p = jnp.stack([
        jnp.concatenate([tg.reshape(batch, nblk, SUBLANES, d_ff)[:, -1, SUBLANES - (CONV_W - 1):],
                         tu.reshape(batch, nblk, SUBLANES, d_ff)[:, -1, SUBLANES - (CONV_W - 1):]], axis=-1)
        for tg, tu in tails_p], axis=0)

    qkv_new = jnp.stack([n_major(q) for q in qkv_s], axis=0).reshape(
        n_a, db, dec_seq, N_GROUPS_A, 3, heads_a, HEAD_DIM)
    win_s = []
    for g in range(N_GROUPS_A):
        c = caches[g]
        pads = [(0, 0, 0)] * c.ndim
        pads[2] = (-dec_seq, dec_seq, 0)
        win_s.append(lax.dynamic_update_slice(lax.pad(c, jnp.zeros((), c.dtype), pads), qkv_new[:, :, :, g, 1:3],
                                              (0, 0, c.shape[2] - dec_seq, 0, 0, 0)))
    kv_sample = n_major(kv_s).reshape(db, dec_seq, 2, heads_b, HEAD_DIM)
    conv_s = jnp.stack([
        jnp.concatenate([tg, tu], axis=-1).reshape(CONV_W - 1, db, 2 * d_ff).transpose(1, 0, 2)
        for tg, tu in tails_s], axis=0)

    y_prompt = y_p.reshape(batch, seq, d)
    y_sample = y_s.reshape(dec_seq, db, d).transpose(1, 0, 2)
    return (y_prompt, y_sample, win_p[0], win_s[0], win_p[1], win_s[1], win_p[2], win_s[2],
            kv_prompt, kv_sample, conv_p, conv_s)
```

```python
import functools

import jax
import jax.numpy as jnp
from jax import lax
from jax.experimental import pallas as pl
from jax.experimental.pallas import tpu as pltpu

F32 = jnp.float32
BF16 = jnp.bfloat16

HEAD_DIM = 128
N_GROUPS_A = 3
WINDOWS = (128, 512, 2048)
DILATIONS = (1, 4, 16)
ROT_DIM = HEAD_DIM // 4
ROPE_THETA = 500000.0
CONV_W = 3
RMS_EPS = 1e-6
NEG_BIG = -1e30
UNROLL_A = 4
K_BLOCK_BIG = 512
BAND = 128
SUBLANES = 8
ADA_ROWS = 16
VMEM_BIG = 56 * 1024 * 1024
VMEM_MID = 48 * 1024 * 1024
VMEM_DOWN = 44 * 1024 * 1024


def _params(sem, vmem=VMEM_MID):
    return pltpu.CompilerParams(dimension_semantics=sem, vmem_limit_bytes=vmem)


def _sigmoid(x):
    return 1.0 / (1.0 + jnp.exp(-x))


def _norm_kernel(x_ref, g_ref, sc_ref, sh_ref, o_ref):
    x = x_ref[...]
    ms = jnp.mean(x * x, axis=-1, keepdims=True)
    y = x * lax.rsqrt(ms + RMS_EPS) * g_ref[...]
    o_ref[...] = (y * (1.0 + sc_ref[...]) + sh_ref[...]).astype(o_ref.dtype)


def _final_norm_kernel(x_ref, g_ref, o_ref):
    x = x_ref[...]
    ms = jnp.mean(x * x, axis=-1, keepdims=True)
    o_ref[...] = (x * lax.rsqrt(ms + RMS_EPS) * g_ref[...]).astype(o_ref.dtype)


def _norm_mod(x, g, sc, sh, *, bm, rows_per_mod):
    m, d = x.shape
    r = sc.shape[1]
    mod_spec = pl.BlockSpec((None, r, d), lambda i: ((i * bm) // rows_per_mod, 0, 0))
    return pl.pallas_call(
        _norm_kernel,
        grid=(m // bm,),
        in_specs=[pl.BlockSpec((bm, d), lambda i: (i, 0)),
                  pl.BlockSpec((1, d), lambda i: (0, 0)),
                  mod_spec, mod_spec],
        out_specs=pl.BlockSpec((bm, d), lambda i: (i, 0)),
        out_shape=jax.ShapeDtypeStruct((m, d), BF16),
        compiler_params=_params(("arbitrary",)),
        name="norm_mod",
    )(x, g, sc, sh)


def _final_norm(x, g, *, bm):
    m, d = x.shape
    return pl.pallas_call(
        _final_norm_kernel,
        grid=(m // bm,),
        in_specs=[pl.BlockSpec((bm, d), lambda i: (i, 0)),
                  pl.BlockSpec((1, d), lambda i: (0, 0))],
        out_specs=pl.BlockSpec((bm, d), lambda i: (i, 0)),
        out_shape=jax.ShapeDtypeStruct((m, d), F32),
        compiler_params=_params(("arbitrary",)),
        name="final_norm",
    )(x, g)


def _cast_weight(w_ref, wb_ref):
    k = w_ref.shape[0]
    chunk = 512 if k % 512 == 0 else k

    def body(c, _):
        r0 = pl.multiple_of(c * chunk, chunk)
        wb_ref[pl.ds(r0, chunk), :] = w_ref[pl.ds(r0, chunk), :].astype(BF16)
        return 0

    lax.fori_loop(0, k // chunk, body, 0)


def _rope_tile(acc, tab_ref):
    cos = tab_ref[:, 0:HEAD_DIM]
    sin = tab_ref[:, HEAD_DIM:2 * HEAD_DIM]
    half = ROT_DIM // 2
    lane = lax.broadcasted_iota(jnp.int32, (acc.shape[0], HEAD_DIM), 1)
    partner = jnp.where(lane < half, lane + half, jnp.where(lane < ROT_DIM, lane - half, lane))
    pieces = []
    for hh in range(acc.shape[1] // HEAD_DIM):
        x = acc[:, hh * HEAD_DIM:(hh + 1) * HEAD_DIM]
        pieces.append(x * cos + jnp.take_along_axis(x, partner, axis=1) * sin)
    return pieces


_N_EXTRA = {"plain": 0, "bias": 1, "resid": 2, "rope": 1}


def _mm_epilogue(acc, extra, o_ref, *, mode, qkv_width, out_scale):
    if mode == "plain":
        if out_scale != 1.0:
            acc = acc * out_scale
        o_ref[...] = acc.astype(o_ref.dtype)
    elif mode == "bias":
        o_ref[...] = (acc + extra[0][...]).astype(o_ref.dtype)
    elif mode == "resid":
        res_ref, gate_ref = extra
        o_ref[...] = (res_ref[...] + gate_ref[...] * acc).astype(o_ref.dtype)
    else:
        bn = acc.shape[1]
        kind = ((pl.program_id(0) * bn) // qkv_width) % 3

        @pl.when(kind == 2)
        def _():
            o_ref[...] = acc.astype(o_ref.dtype)

        @pl.when(kind != 2)
        def _():
            for hh, piece in enumerate(_rope_tile(acc, extra[0])):
                o_ref[:, hh * HEAD_DIM:(hh + 1) * HEAD_DIM] = piece.astype(o_ref.dtype)


def _mm_kernel(*refs, mode, silu_in, qkv_width, out_scale, side):
    ne = _N_EXTRA[mode]
    x_ref, w_ref = refs[0], refs[1]
    extra = refs[2:2 + ne]
    pos = 2 + ne
    if side:
        xs_ref, extra_s = refs[pos], refs[pos + 1:pos + 1 + ne]
        pos += 1 + ne
    o_ref = refs[pos]
    os_ref = refs[pos + 1] if side else None
    wb_ref = refs[-1]
    epilogue = functools.partial(_mm_epilogue, mode=mode, qkv_width=qkv_width, out_scale=out_scale)

    @pl.when(pl.program_id(1) == 0)
    def _():
        _cast_weight(w_ref, wb_ref)
        if side:
            acc_s = jnp.dot(xs_ref[...].astype(BF16), wb_ref[...], preferred_element_type=F32)
            epilogue(acc_s, extra_s, os_ref)

    x = x_ref[...]
    if silu_in:
        x = x * _sigmoid(x)
    epilogue(jnp.dot(x.astype(BF16), wb_ref[...], preferred_element_type=F32), extra, o_ref)


def _matmul(x, w, layer, *, bm, bn, out_dtype, mode="plain", extras=(), extra_specs=(),
            silu_in=False, qkv_width=0, out_scale=1.0, side=None, vmem=VMEM_BIG, name="matmul"):
    m, k = x.shape
    n = w.shape[2]
    kern = functools.partial(_mm_kernel, mode=mode, silu_in=silu_in, qkv_width=qkv_width, out_scale=out_scale,
                             side=side is not None)
    in_specs = [pl.BlockSpec((bm, k), lambda j, i: (i, 0)),
                pl.BlockSpec((None, k, bn), lambda j, i: (layer, 0, j)),
                *extra_specs]
    args = [x, w, *extras]
    out_specs = pl.BlockSpec((bm, bn), lambda j, i: (i, j))
    out_shape = jax.ShapeDtypeStruct((m, n), out_dtype)
    if side is not None:
        xs, extras_s, specs_s, dtype_s = side
        ms = xs.shape[0]
        in_specs += [pl.BlockSpec((ms, k), lambda j, i: (0, 0)), *specs_s]
        args += [xs, *extras_s]
        out_specs = [out_specs, pl.BlockSpec((ms, bn), lambda j, i: (0, j))]
        out_shape = [out_shape, jax.ShapeDtypeStruct((ms, n), dtype_s)]
    return pl.pallas_call(
        kern,
        grid=(n // bn, m // bm),
        in_specs=in_specs,
        out_specs=out_specs,
        out_shape=out_shape,
        scratch_shapes=[pltpu.VMEM((k, bn), BF16)],
        compiler_params=_params(("arbitrary", "arbitrary"), vmem),
        name=name,
    )(*args)


def _mm_ktiled_kernel(*refs, nk_big, nk_small, side):
    nw = 2 if nk_small else 1
    per = nw + 2
    w_refs = refs[:nw]
    groups = [refs[nw:nw + per]] + ([refs[nw + per:nw + 2 * per]] if side else [])
    outs = refs[nw + len(groups) * per:]
    kk = pl.program_id(2)
    first_rows = pl.program_id(1) == 0

    def each_group(fn):
        fn(groups[0], outs[0])
        if side:
            @pl.when(first_rows)
            def _():
                fn(groups[1], outs[1])

    @pl.when(kk == 0)
    def _():
        wb = w_refs[0][...].astype(BF16)

        def start(g, o_ref):
            o_ref[...] = jnp.dot(g[0][...], wb, preferred_element_type=F32)
        each_group(start)

    @pl.when(jnp.logical_and(kk > 0, kk < nk_big))
    def _():
        wb = w_refs[0][...].astype(BF16)

        def add(g, o_ref):
            o_ref[...] += jnp.dot(g[0][...], wb, preferred_element_type=F32)
        each_group(add)

    if nk_small:
        @pl.when(kk >= nk_big)
        def _():
            ws = w_refs[1][...].astype(BF16)

            def add_small(g, o_ref):
                o_ref[...] += jnp.dot(g[1][...], ws, preferred_element_type=F32)
            each_group(add_small)

    @pl.when(kk == nk_big + nk_small - 1)
    def _():
        def finish(g, o_ref):
            o_ref[...] = g[nw][...] + g[nw + 1][...] * o_ref[...]
        each_group(finish)


def _matmul_ktiled_resid(x, w, layer, res, gate, *, bm, bn, rows_per_mod, side=None, name="down"):
    m, k = x.shape
    n = w.shape[2]
    r = gate.shape[1]
    bk_big = min(k, K_BLOCK_BIG)
    nk_big = k // bk_big
    rem = k - nk_big * bk_big
    bk_small = _pick(rem, (256, 128)) if rem else 0
    nk_small = rem // bk_small if rem else 0
    small0 = (nk_big * bk_big) // bk_small if rem else 0
    assert rem == 0 or (nk_big * bk_big) % bk_small == 0

    def big(kk):
        return jnp.minimum(kk, nk_big - 1)

    def small(kk):
        return small0 + jnp.maximum(kk - nk_big, 0)

    in_specs = [pl.BlockSpec((None, bk_big, bn), lambda j, i, kk: (layer, big(kk), j))]
    args = [w]
    if nk_small:
        in_specs.append(pl.BlockSpec((None, bk_small, bn), lambda j, i, kk: (layer, small(kk), j)))
        args.append(w)
    in_specs.append(pl.BlockSpec((bm, bk_big), lambda j, i, kk: (i, big(kk))))
    args.append(x)
    if nk_small:
        in_specs.append(pl.BlockSpec((bm, bk_small), lambda j, i, kk: (i, small(kk))))
        args.append(x)
    in_specs += [pl.BlockSpec((bm, bn), lambda j, i, kk: (i, j), pipeline_mode=pl.Buffered(1)),
                 pl.BlockSpec((None, r, bn), lambda j, i, kk: ((i * bm) // rows_per_mod, 0, j))]
    args += [res, gate]
    out_specs = pl.BlockSpec((bm, bn), lambda j, i, kk: (i, j))
    out_shape = jax.ShapeDtypeStruct((m, n), F32)
    if side is not None:
        xs, res_s, gate_s = side
        ms = xs.shape[0]
        in_specs.append(pl.BlockSpec((ms, bk_big), lambda j, i, kk: (0, big(kk))))
        args.append(xs)
        if nk_small:
            in_specs.append(pl.BlockSpec((ms, bk_small), lambda j, i, kk: (0, small(kk))))
            args.append(xs)
        in_specs += [pl.BlockSpec((ms, bn), lambda j, i, kk: (0, j)),
                     pl.BlockSpec((None, ms, bn), lambda j, i, kk: (0, 0, j))]
        args += [res_s, gate_s]
        out_specs = [out_specs, pl.BlockSpec((ms, bn), lambda j, i, kk: (0, j))]
        out_shape = [out_shape, jax.ShapeDtypeStruct((ms, n), F32)]
    return pl.pallas_call(
        functools.partial(_mm_ktiled_kernel, nk_big=nk_big, nk_small=nk_small, side=side is not None),
        grid=(n // bn, m // bm, nk_big + nk_small),
        in_specs=in_specs,
        out_specs=out_specs,
        out_shape=out_shape,
        compiler_params=_params(("arbitrary", "arbitrary", "arbitrary"), VMEM_DOWN),
        name=name,
    )(*args)


def _ffn_rows(h_ref, act_ref, stg_ref, stu_ref, ug, uu, wgb, wub, conv_refs, *, rows, halo, tap):
    cwg_ref, cwu_ref, cbg_ref, cbu_ref = conv_refs
    h = h_ref[...]
    ug[halo:halo + rows, :] = jnp.dot(h, wgb[...], preferred_element_type=F32)
    uu[halo:halo + rows, :] = jnp.dot(h, wub[...], preferred_element_type=F32)

    def conv(u, cw_ref, cb_ref):
        y = cb_ref[...] + cw_ref[0:1, :] * u[halo - 2 * tap:halo - 2 * tap + rows, :]
        y = y + cw_ref[1:2, :] * u[halo - tap:halo - tap + rows, :]
        return y + cw_ref[2:3, :] * u[halo:halo + rows, :]

    gate = conv(ug, cwg_ref, cbg_ref)
    up = conv(uu, cwu_ref, cbu_ref)
    act_ref[...] = (gate * _sigmoid(gate) * up).astype(act_ref.dtype)
    tail_g = ug[rows:rows + halo, :]
    tail_u = uu[rows:rows + halo, :]
    stg_ref[...] = tail_g
    stu_ref[...] = tail_u
    ug[0:halo, :] = tail_g
    uu[0:halo, :] = tail_u


def _ffn_up_kernel(*refs, bm, halo, blocks_per_seq, side):
    h_ref, wg_ref, wu_ref = refs[0:3]
    conv_refs = refs[3:7]
    pos = 7
    if side:
        hs_ref, pgs_ref, pus_ref = refs[pos:pos + 3]
        pos += 3
    act_ref, stg_ref, stu_ref = refs[pos:pos + 3]
    pos += 3
    if side:
        acts_ref, stgs_ref, stus_ref = refs[pos:pos + 3]
        pos += 3
    wgb, wub, ug, uu = refs[pos:pos + 4]
    i = pl.program_id(1)

    @pl.when(i == 0)
    def _():
        _cast_weight(wg_ref, wgb)
        _cast_weight(wu_ref, wub)
        if side:
            halo_s, tap_s = side
            ugs, uus = refs[pos + 4:pos + 6]
            ugs[0:halo_s, :] = pgs_ref[...]
            uus[0:halo_s, :] = pus_ref[...]
            _ffn_rows(hs_ref, acts_ref, stgs_ref, stus_ref, ugs, uus, wgb, wub, conv_refs,
                      rows=hs_ref.shape[0], halo=halo_s, tap=tap_s)

    @pl.when(i % blocks_per_seq == 0)
    def _():
        ug[0:halo, :] = jnp.zeros((halo, ug.shape[1]), F32)
        uu[0:halo, :] = jnp.zeros((halo, uu.shape[1]), F32)

    _ffn_rows(h_ref, act_ref, stg_ref, stu_ref, ug, uu, wgb, wub, conv_refs, rows=bm, halo=halo, tap=1)


def _ffn_up(h, w_up, w_conv, b_conv3, layer, *, bm, bn, rows_per_seq, side=None):
    m, d = h.shape
    d_ff = w_up.shape[2] // 2
    nj = d_ff // bn
    halo = SUBLANES
    in_specs = [pl.BlockSpec((bm, d), lambda j, i: (i, 0)),
                pl.BlockSpec((None, d, bn), lambda j, i: (layer, 0, j)),
                pl.BlockSpec((None, d, bn), lambda j, i: (layer, 0, j + nj)),
                pl.BlockSpec((None, CONV_W, bn), lambda j, i: (layer, 0, j)),
                pl.BlockSpec((None, CONV_W, bn), lambda j, i: (layer, 0, j + nj)),
                pl.BlockSpec((None, 1, bn), lambda j, i: (layer, 0, j)),
                pl.BlockSpec((None, 1, bn), lambda j, i: (layer, 0, j + nj))]
    args = [h, w_up, w_up, w_conv, w_conv, b_conv3, b_conv3]
    st_spec = pl.BlockSpec((None, halo, bn), lambda j, i: (i, 0, j))
    out_specs = [pl.BlockSpec((bm, bn), lambda j, i: (i, j)), st_spec, st_spec]
    out_shape = [jax.ShapeDtypeStruct((m, d_ff), BF16),
                 jax.ShapeDtypeStruct((m // bm, halo, d_ff), F32),
                 jax.ShapeDtypeStruct((m // bm, halo, d_ff), F32)]
    scratch = [pltpu.VMEM((d, bn), BF16), pltpu.VMEM((d, bn), BF16),
               pltpu.VMEM((bm + halo, bn), F32), pltpu.VMEM((bm + halo, bn), F32)]
    side_cfg = None
    if side is not None:
        hs, prev, tap_s = side
        ms = hs.shape[0]
        halo_s = (CONV_W - 1) * tap_s
        side_cfg = (halo_s, tap_s)
        in_specs += [pl.BlockSpec((ms, d), lambda j, i: (0, 0)),
                     pl.BlockSpec((halo_s, bn), lambda j, i: (0, j)),
                     pl.BlockSpec((halo_s, bn), lambda j, i: (0, j + nj))]
        args += [hs, prev, prev]
        sts_spec = pl.BlockSpec((halo_s, bn), lambda j, i: (0, j))
        out_specs += [pl.BlockSpec((ms, bn), lambda j, i: (0, j)), sts_spec, sts_spec]
        out_shape += [jax.ShapeDtypeStruct((ms, d_ff), BF16),
                      jax.ShapeDtypeStruct((halo_s, d_ff), F32), jax.ShapeDtypeStruct((halo_s, d_ff), F32)]
        scratch += [pltpu.VMEM((ms + halo_s, bn), F32), pltpu.VMEM((ms + halo_s, bn), F32)]
    kern = functools.partial(_ffn_up_kernel, bm=bm, halo=halo, blocks_per_seq=rows_per_seq // bm, side=side_cfg)
    return pl.pallas_call(
        kern,
        grid=(nj, m // bm),
        in_specs=in_specs,
        out_specs=out_specs,
        out_shape=out_shape,
        scratch_shapes=scratch,
        compiler_params=_params(("arbitrary", "arbitrary"), VMEM_BIG),
        name="ffn_up",
    )(*args)


def _attn_a_prompt_kernel(q0, k0, v0, q1, k1, v1, q2, k2, v2, o_ref, og, lg, *, seq):
    qkv = ((q0, k0, v0), (q1, k1, v1), (q2, k2, v2))
    scale = HEAD_DIM ** -0.5
    row = lax.broadcasted_iota(jnp.int32, (BAND, 2 * BAND), 0)
    col = lax.broadcasted_iota(jnp.int32, (BAND, 2 * BAND), 1)
    cur_ok = jnp.logical_and(col >= BAND, col - BAND <= row)
    prev_ok = jnp.logical_and(col < BAND, col >= row)
    dn = (((1,), (1,)), ((), ()))

    for g in range(N_GROUPS_A):
        dil = DILATIONS[g]
        nb = seq // dil // BAND
        q_ref, k_ref, v_ref = qkv[g]
        unroll = UNROLL_A if (dil * nb) % UNROLL_A == 0 else 1

        def rows(start, dil=dil):
            if dil == 1:
                return pl.ds(pl.multiple_of(start, BAND), BAND)
            return pl.ds(start, BAND, stride=dil)

        def body(it, _, g=g, dil=dil, nb=nb, q_ref=q_ref, k_ref=k_ref, v_ref=v_ref, rows=rows, unroll=unroll):
            starts, oks, scores, vbands = [], [], [], []
            for u in range(unroll):
                idx = it * unroll + u
                blk = idx % nb
                start = idx // nb + blk * (BAND * dil)
                has_prev = blk > 0
                pstart = jnp.where(has_prev, start - BAND * dil, start)
                q = q_ref[rows(start), :].astype(BF16)
                kband = jnp.concatenate([k_ref[rows(pstart), :], k_ref[rows(start), :]], axis=0).astype(BF16)
                vbands.append(jnp.concatenate([v_ref[rows(pstart), :], v_ref[rows(start), :]], axis=0).astype(BF16))
                scores.append(lax.dot_general(q, kband, dn, preferred_element_type=F32))
                starts.append(start)
                oks.append(jnp.logical_or(cur_ok, jnp.logical_and(prev_ok, has_prev)))
            probs, dens, lses = [], [], []
            for u in range(unroll):
                sc = jnp.where(oks[u], scores[u] * scale, NEG_BIG)
                mx = jnp.max(sc, axis=1, keepdims=True)
                p = jnp.exp(sc - mx)
                den = jnp.sum(p, axis=1, keepdims=True)
                probs.append(p.astype(BF16))
                dens.append(den)
                lses.append(mx + jnp.log(den))
            for u in range(unroll):
                o = jnp.dot(probs[u], vbands[u], preferred_element_type=F32)
                og[g, rows(starts[u]), :] = o / dens[u]
                lg[g, rows(starts[u]), :] = jnp.broadcast_to(lses[u], (BAND, HEAD_DIM))
            return 0

        lax.fori_loop(0, dil * nb // unroll, body, 0)

    chunk = 256

    def comb(c, _):
        r0 = pl.multiple_of(c * chunk, chunk)
        l0 = lg[0, pl.ds(r0, chunk), :]
        l1 = lg[1, pl.ds(r0, chunk), :]
        l2 = lg[2, pl.ds(r0, chunk), :]
        mx = jnp.maximum(jnp.maximum(l0, l1), l2)
        e0 = jnp.exp(l0 - mx)
        e1 = jnp.exp(l1 - mx)
        e2 = jnp.exp(l2 - mx)
        tot = e0 + e1 + e2
        o = (og[0, pl.ds(r0, chunk), :] * (e0 / tot) + og[1, pl.ds(r0, chunk), :] * (e1 / tot)
             + og[2, pl.ds(r0, chunk), :] * (e2 / tot))
        o_ref[pl.ds(r0, chunk), :] = o.astype(o_ref.dtype)
        return 0

    lax.fori_loop(0, seq // chunk, comb, 0)


def _attn_a_prompt(qkv, *, batch, seq, heads):
    m = qkv.shape[0]
    specs = []
    for g in range(N_GROUPS_A):
        for t in range(3):
            specs.append(pl.BlockSpec((seq, HEAD_DIM),
                                      lambda n, h, g=g, t=t: (n, (g * 3 + t) * heads + h)))
    return pl.pallas_call(
        functools.partial(_attn_a_prompt_kernel, seq=seq),
        grid=(batch, heads),
        in_specs=specs,
        out_specs=pl.BlockSpec((seq, HEAD_DIM), lambda n, h: (n, h)),
        out_shape=jax.ShapeDtypeStruct((m, heads * HEAD_DIM), BF16),
        scratch_shapes=[pltpu.VMEM((N_GROUPS_A, seq, HEAD_DIM), F32),
                        pltpu.VMEM((N_GROUPS_A, seq, HEAD_DIM), F32)],
        compiler_params=_params(("arbitrary", "arbitrary")),
        name="attn_a_prompt",
    )(*([qkv] * 9))


def _attn_a_sample_kernel(qn_ref, c0_ref, c1_ref, c2_ref, o_ref, *, heads, dec_seq):
    width = heads * HEAD_DIM
    scale = HEAD_DIM ** -0.5
    caches = (c0_ref, c1_ref, c2_ref)
    hrow = lax.broadcasted_iota(jnp.int32, (heads, width), 0)
    hcol = lax.broadcasted_iota(jnp.int32, (heads, width), 1) // HEAD_DIM
    head_mask = hrow == hcol
    key_idx = lax.broadcasted_iota(jnp.int32, (heads, BAND), 1)
    dn = (((1,), (1,)), ((), ()))

    def split_heads(x):
        y = pltpu.einshape("mrd->rmd", x)
        k = jnp.concatenate([y[h] for h in range(heads)], axis=1).astype(BF16)
        v = jnp.concatenate([y[heads + h] for h in range(heads)], axis=1).astype(BF16)
        return k, v

    rh = 2 * heads
    kv_bufs = [[split_heads(caches[0][...])]]
    for g in range(1, N_GROUPS_A):
        kv_bufs.append([split_heads(caches[g][:, t * rh:(t + 1) * rh, :]) for t in range(dec_seq)])

    for t in range(dec_seq):
        outs, lses = [], []
        for g in range(N_GROUPS_A):
            base = g * 3 * width
            q_row = qn_ref[t:t + 1, base:base + width]
            qbd = jnp.where(head_mask, jnp.broadcast_to(q_row, (heads, width)), 0.0)
            kb, vb = kv_bufs[g][0 if g == 0 else t]
            s_buf = lax.dot_general(qbd.astype(BF16), kb, dn, preferred_element_type=F32) * scale
            if g == 0:
                s_buf = jnp.where(key_idx >= t, s_buf, NEG_BIG)
                new_rows = range(t + 1)
            else:
                new_rows = (t,)
            s_new = []
            for tn in new_rows:
                k_row = qn_ref[tn:tn + 1, base + width:base + 2 * width]
                s_new.append(jnp.sum(qbd * k_row, axis=1, keepdims=True) * scale)
            mx = jnp.max(s_buf, axis=1, keepdims=True)
            for s in s_new:
                mx = jnp.maximum(mx, s)
            p_buf = jnp.exp(s_buf - mx)
            den = jnp.sum(p_buf, axis=1, keepdims=True)
            o = jnp.dot(p_buf.astype(BF16), vb, preferred_element_type=F32)
            for tn, s in zip(new_rows, s_new):
                p = jnp.exp(s - mx)
                den = den + p
                o = o + p * qn_ref[tn:tn + 1, base + 2 * width:base + 3 * width]
            o = jnp.where(head_mask, o / den, 0.0)
            lse = jnp.where(head_mask, jnp.broadcast_to(mx + jnp.log(den), (heads, width)), 0.0)
            outs.append(jnp.sum(o, axis=0, keepdims=True))
            lses.append(jnp.sum(lse, axis=0, keepdims=True))
        mx = jnp.maximum(jnp.maximum(lses[0], lses[1]), lses[2])
        es = [jnp.exp(l - mx) for l in lses]
        tot = es[0] + es[1] + es[2]
        row = outs[0] * (es[0] / tot) + outs[1] * (es[1] / tot) + outs[2] * (es[2] / tot)
        o_ref[t:t + 1, :] = row.astype(o_ref.dtype)


def _attn_a_sample(qn, caches, layer, *, heads):
    db, dec_seq, _ = qn.shape
    width = heads * HEAD_DIM
    views, specs = [], []
    rh = 2 * heads
    for g in range(N_GROUPS_A):
        c = caches[g]
        nlay, _, nbuf = c.shape[:3]
        dil = DILATIONS[g]
        views.append(c.reshape(nlay, db, nbuf // dil, dil * rh, HEAD_DIM))
        need = rh if g == 0 else dec_seq * rh
        blk_r = need if need == dil * rh else -(-need // SUBLANES) * SUBLANES
        specs.append(pl.BlockSpec((None, None, nbuf // dil, blk_r, HEAD_DIM), lambda n: (layer, n, 0, 0, 0)))
    return pl.pallas_call(
        functools.partial(_attn_a_sample_kernel, heads=heads, dec_seq=dec_seq),
        grid=(db,),
        in_specs=[pl.BlockSpec((None, dec_seq, qn.shape[2]), lambda n: (n, 0, 0)), *specs],
        out_specs=pl.BlockSpec((None, dec_seq, width), lambda n: (n, 0, 0)),
        out_shape=jax.ShapeDtypeStruct((db, dec_seq, width), F32),
        compiler_params=_params(("arbitrary",)),
        name="attn_a_sample",
    )(qn, *views)


LOG2E = 1.4426950408889634
LN2 = 0.6931471805599453
SB_QSCALE = HEAD_DIM ** -0.5 * LOG2E


def _sb_blocks(x2s, tri, masks, vs):
    sps = []
    for x2, mask in zip(x2s, masks):
        neg_abs = pltpu.bitcast(pltpu.bitcast(x2, jnp.uint32) | jnp.uint32(0x80000000), F32)
        sp = LN2 * jnp.maximum(x2, 0.0) + jnp.log(1.0 + jnp.exp2(neg_abs))
        sps.append(sp if mask is None else jnp.where(mask, sp, 0.0))
    incls = [jnp.dot(sp.astype(BF16), tri, preferred_element_type=F32) for sp in sps]
    ws = []
    for x2, incl, mask in zip(x2s, incls, masks):
        a = jnp.exp2(x2 - LOG2E * incl)
        ws.append((a if mask is None else jnp.where(mask, a, 0.0)).astype(BF16))
    pvs = [jnp.dot(a, v, preferred_element_type=F32) for a, v in zip(ws, vs)]
    return [(pv, jnp.sum(sp, axis=1, keepdims=True)) for pv, sp in zip(pvs, sps)]


def _sb_prompt_kernel(q_ref, k_ref, v_ref, b_ref, tri_ref, o_ref, kb_ref, vb_ref, *, seq, blk, hpb):
    nblk = seq // blk
    chunk = 512 if seq % 512 == 0 else seq

    def cast(c, _):
        r0 = pl.multiple_of(c * chunk, chunk)
        kb_ref[pl.ds(r0, chunk), :] = k_ref[pl.ds(r0, chunk), :].astype(BF16)
        vb_ref[pl.ds(r0, chunk), :] = v_ref[pl.ds(r0, chunk), :].astype(BF16)
        return 0

    lax.fori_loop(0, seq // chunk, cast, 0)
    row = lax.broadcasted_iota(jnp.int32, (blk, blk), 0)
    col = lax.broadcasted_iota(jnp.int32, (blk, blk), 1)
    diag_mask = col < row
    dn = (((1,), (1,)), ((), ()))
    lanes = [slice(hh * HEAD_DIM, (hh + 1) * HEAD_DIM) for hh in range(hpb)]

    def q_block(qb, _):
        q0 = pl.multiple_of(qb * blk, blk)
        qs = [q_ref[pl.ds(q0, blk), ln] for ln in lanes]

        def blocks(kbs, mask):
            k0s = [pl.multiple_of(kb * blk, blk) for kb in kbs]
            x2s = [lax.dot_general(qs[hh], kb_ref[pl.ds(k0, blk), lanes[hh]], dn,
                                   preferred_element_type=F32) + b_ref[hh]
                   for hh in range(hpb) for k0 in k0s]
            vs = [vb_ref[pl.ds(k0, blk), lanes[hh]] for hh in range(hpb) for k0 in k0s]
            out = _sb_blocks(x2s, tri_ref[...], [mask] * len(x2s), vs)
            return [out[hh * len(kbs):(hh + 1) * len(kbs)] for hh in range(hpb)]

        state = tuple(res[0] for res in blocks([qb], diag_mask))

        def pair(it, state):
            res = blocks([qb - 1 - 2 * it, qb - 2 - 2 * it], None)
            out = []
            for (acc, carry), ((pv0, m0), (pv1, m1)) in zip(state, res):
                acc = acc + jnp.exp(-carry) * pv0 + jnp.exp(-(carry + m0)) * pv1
                out.append((acc, carry + m0 + m1))
            return tuple(out)

        state = lax.fori_loop(0, qb // 2, pair, state)

        def last(state):
            res = blocks([0], None)
            return tuple((acc + jnp.exp(-carry) * pv, carry + m)
                         for (acc, carry), ((pv, m),) in zip(state, res))

        state = lax.cond(qb % 2 == 1, last, lambda st: st, state)
        for hh, (acc, _) in enumerate(state):
            o_ref[pl.ds(q0, blk), lanes[hh]] = acc.astype(o_ref.dtype)
        return 0

    lax.fori_loop(0, nblk, q_block, 0)


def _tri(blk):
    r = lax.broadcasted_iota(jnp.int32, (blk, blk), 0)
    c = lax.broadcasted_iota(jnp.int32, (blk, blk), 1)
    return (r >= c).astype(BF16)


def _sb_prompt(q, kv, bias, *, batch, seq, heads, blk, hpb=4):
    m = q.shape[0]
    hpb = hpb if heads % hpb == 0 else 1
    w = hpb * HEAD_DIM
    bias_b = jnp.broadcast_to(LOG2E * bias.astype(F32)[:, None, None], (heads, 1, blk))
    return pl.pallas_call(
        functools.partial(_sb_prompt_kernel, seq=seq, blk=blk, hpb=hpb),
        grid=(batch, heads // hpb),
        in_specs=[pl.BlockSpec((seq, w), lambda n, h: (n, h)),
                  pl.BlockSpec((seq, w), lambda n, h: (n, h)),
                  pl.BlockSpec((seq, w), lambda n, h: (n, heads // hpb + h)),
                  pl.BlockSpec((hpb, 1, blk), lambda n, h: (h, 0, 0)),
                  pl.BlockSpec((blk, blk), lambda n, h: (0, 0))],
        out_specs=pl.BlockSpec((seq, w), lambda n, h: (n, h)),
        out_shape=jax.ShapeDtypeStruct((m, heads * HEAD_DIM), BF16),
        scratch_shapes=[pltpu.VMEM((seq, w), BF16), pltpu.VMEM((seq, w), BF16)],
        compiler_params=_params(("arbitrary", "arbitrary")),
        name="sb_prompt",
    )(q, kv, kv, bias_b, _tri(blk))


QROWS = SUBLANES


def _sb_sample_kernel(pt_ref, q_ref, new_ref, page_a_ref, page_b_ref, b_ref, tri_ref, o_ref,
                      q3_ref, acc_ref, carry_ref, *, heads, dec_seq, page):
    del pt_ref
    p = pl.program_id(1)
    rows = heads * QROWS

    def block(kv_ref, mask):
        x = kv_ref[...].astype(BF16).reshape(page, 2 * heads, HEAD_DIM)
        k = pltpu.einshape("khd->hkd", x[:, 0:heads, :])
        v = pltpu.einshape("khd->hkd", x[:, heads:2 * heads, :])
        s = jnp.einsum("htd,hkd->htk", q3_ref[...].astype(BF16), k, preferred_element_type=F32)
        x2 = s.reshape(rows, page) + b_ref[...]
        neg_abs = pltpu.bitcast(pltpu.bitcast(x2, jnp.uint32) | jnp.uint32(0x80000000), F32)
        sp = LN2 * jnp.maximum(x2, 0.0) + jnp.log(1.0 + jnp.exp2(neg_abs))
        if mask is not None:
            sp = jnp.where(mask, sp, 0.0)
        incl = jnp.dot(sp.astype(BF16), tri_ref[...], preferred_element_type=F32)
        a = jnp.exp2(x2 - LOG2E * incl)
        if mask is not None:
            a = jnp.where(mask, a, 0.0)
        pv = jnp.einsum("htk,hkd->htd", a.astype(BF16).reshape(heads, QROWS, page), v,
                        preferred_element_type=F32).reshape(rows, HEAD_DIM)
        return pv, jnp.sum(sp, axis=1, keepdims=True)

    def accumulate(results):
        carry = carry_ref[:, 0:1]
        acc = acc_ref[...]
        for pv, mass in results:
            acc = acc + jnp.exp(-carry) * pv
            carry = carry + mass
        acc_ref[...] = acc
        carry_ref[...] = jnp.broadcast_to(carry, carry_ref.shape)

    @pl.when(p == 0)
    def _():
        q3_ref[...] = jnp.zeros_like(q3_ref)
        for h in range(heads):
            q3_ref[h, 0:dec_seq, :] = q_ref[:, h * HEAD_DIM:(h + 1) * HEAD_DIM]
        acc_ref[...] = jnp.zeros_like(acc_ref)
        carry_ref[...] = jnp.zeros_like(carry_ref)
        t_of_row = lax.broadcasted_iota(jnp.int32, (rows, page), 0) % QROWS
        key = lax.broadcasted_iota(jnp.int32, (rows, page), 1)
        accumulate([block(new_ref, key < t_of_row)])

    @pl.when(p > 0)
    def _():
        accumulate([block(page_a_ref, None), block(page_b_ref, None)])

    @pl.when(p == pl.num_programs(1) - 1)
    def _():
        for h in range(heads):
            o_ref[:, h * HEAD_DIM:(h + 1) * HEAD_DIM] = acc_ref[h * QROWS:h * QROWS + dec_seq, :]


def _sb_sample(q, kv_new_pad, pages, page_table, bias, *, heads, page):
    db, dec_seq, width = q.shape
    n_pages = page_table.shape[1]
    assert dec_seq <= QROWS and n_pages % 2 == 0
    rows = heads * QROWS
    prow = page * 2 * heads
    bias_b = jnp.broadcast_to(LOG2E * jnp.repeat(bias.astype(F32), QROWS)[:, None], (rows, page))

    def page_map(which):
        return lambda n, p, pt: (pt[n, n_pages - 2 * jnp.maximum(p, 1) + 1 - which], 0, 0)

    grid_spec = pltpu.PrefetchScalarGridSpec(
        num_scalar_prefetch=1,
        grid=(db, n_pages // 2 + 1),
        in_specs=[pl.BlockSpec((None, dec_seq, width), lambda n, p, pt: (n, 0, 0)),
                  pl.BlockSpec((None, prow, HEAD_DIM), lambda n, p, pt: (n, 0, 0)),
                  pl.BlockSpec((None, prow, HEAD_DIM), page_map(0)),
                  pl.BlockSpec((None, prow, HEAD_DIM), page_map(1)),
                  pl.BlockSpec((rows, page), lambda n, p, pt: (0, 0)),
                  pl.BlockSpec((page, page), lambda n, p, pt: (0, 0))],
        out_specs=pl.BlockSpec((None, dec_seq, width), lambda n, p, pt: (n, 0, 0)),
        scratch_shapes=[pltpu.VMEM((heads, QROWS, HEAD_DIM), F32), pltpu.VMEM((rows, HEAD_DIM), F32),
                        pltpu.VMEM((rows, HEAD_DIM), F32)],
    )
    return pl.pallas_call(
        functools.partial(_sb_sample_kernel, heads=heads, dec_seq=dec_seq, page=page),
        grid_spec=grid_spec,
        out_shape=jax.ShapeDtypeStruct((db, dec_seq, width), F32),
        compiler_params=_params(("arbitrary", "arbitrary")),
        name="sb_sample",
    )(page_table, q, kv_new_pad, pages, pages, bias_b, _tri(page))


def _rope_table(pos):
    half = ROT_DIM // 2
    inv = ROPE_THETA ** (-jnp.arange(half, dtype=F32) / half)
    ang = pos.astype(F32)[:, None] * inv[None, :]
    cos, sin = jnp.cos(ang), jnp.sin(ang)
    t = pos.shape[0]
    ones = jnp.ones((t, HEAD_DIM - ROT_DIM), F32)
    zeros = jnp.zeros((t, HEAD_DIM - ROT_DIM), F32)
    return jnp.concatenate([cos, cos, ones, -sin, sin, zeros], axis=1)


def _pick(v, cands):
    for c in cands:
        if v % c == 0:
            return c
    return v


def _trunk(xp, xs, mods_p, mods_s, modkv_p, modkv_s, tabs_p, tabs_s, prev, cfg, weights):
    (norm_attn_g, norm_ffn_g, w_qkv_a, w_o_a, w_q_b, w_o_b, sb_bias, norm_kv_g, w_kv,
     w_up, w_conv, b_conv3, w_down, norm_final_g) = weights
    d = xp.shape[1]
    ms = xs.shape[0]
    bm, bmn, seq = cfg["bm"], cfg["bm_norm"], cfg["seq"]
    depth = w_up.shape[0]
    n_a = w_qkv_a.shape[0]
    heads_a = w_o_a.shape[1] // HEAD_DIM
    d_ff = w_down.shape[1]

    def norm_both(g, sc_p, sh_p, sc_s, sh_s):
        return (_norm_mod(xp, g, sc_p, sh_p, bm=bmn, rows_per_mod=seq),
                _norm_mod(xs, g, sc_s, sh_s, bm=ms, rows_per_mod=ms))

    def tile(n, cands=(512, 256, 128)):
        return _pick(n, cands)

    def side_rows(bn):
        return (pl.BlockSpec((ms, bn), lambda j, i: (0, j)), pl.BlockSpec((None, ms, bn), lambda j, i: (0, 0, j)))

    qkv_p, qkv_s, tails_p, tails_s = [], [], [], []
    kv_p = kv_s = None
    for l in range(depth):
        sh1, sc1, g1, sh2, sc2, g2 = mods_p[l]
        sh1s, sc1s, g1s, sh2s, sc2s, g2s = mods_s[l]
        hp, hs = norm_both(norm_attn_g[l][None], sc1, sh1, sc1s, sh1s)
        if l < n_a:
            bn = tile(heads_a * HEAD_DIM)
            qp, qs = _matmul(hp, w_qkv_a, l, bm=bm, bn=bn, out_dtype=F32, mode="rope",
                             extras=(tabs_p,), extra_specs=(pl.BlockSpec((bm, 2 * HEAD_DIM), cfg["tab_map"]),),
                             side=(hs, (tabs_s,), (pl.BlockSpec((ms, 2 * HEAD_DIM), lambda j, i: (0, 0)),), F32),
                             qkv_width=heads_a * HEAD_DIM, name="qkv")
            qkv_p.append(qp)
            qkv_s.append(qs)
            op, os_ = cfg["attn_a_p"](qp), cfg["attn_a_s"](qs, l)
            w_o, lo = w_o_a, l
        else:
            j = l - n_a
            qp, qs = _matmul(hp, w_q_b, j, bm=bm, bn=tile(w_q_b.shape[2]), out_dtype=BF16,
                             out_scale=SB_QSCALE, side=(hs, (), (), F32), name="q_b")
            op, os_ = cfg["attn_b_p"](qp, kv_p, sb_bias[j]), cfg["attn_b_s"](qs, kv_s, sb_bias[j])
            w_o, lo = w_o_b, j
        bn = tile(d)
        xp, xs = _matmul(op, w_o, lo, bm=bm, bn=bn, out_dtype=F32, mode="resid", extras=(xp, g1),
                         extra_specs=(pl.BlockSpec((bm, bn), lambda j, i: (i, j)),
                                      pl.BlockSpec((None, 1, bn), lambda j, i: ((i * bm) // seq, 0, j))),
                         side=(os_, (xs, g1s), side_rows(bn), F32), name="attn_out")
        hp, hs = norm_both(norm_ffn_g[l][None], sc2, sh2, sc2s, sh2s)
        act_p, tg, tu, act_s, tgs, tus = _ffn_up(hp, w_up, w_conv, b_conv3, l, bm=bm, bn=_pick(d_ff, (256, 128)),
                                                 rows_per_seq=seq, side=(hs, prev[l], cfg["tap_s"]))
        tails_p.append((tg, tu))
        tails_s.append((tgs, tus))
        xp, xs = _matmul_ktiled_resid(act_p, w_down, l, xp, g2, bm=cfg["bm_down"],
                                      bn=_pick(d, (cfg["bn_down"], 512, 256, 128)), rows_per_mod=seq,
                                      side=(act_s, xs, g2s))
        if l == n_a - 1:
            hp, hs = norm_both(norm_kv_g[None], modkv_p[1], modkv_p[0], modkv_s[1], modkv_s[0])
            kv_p, kv_s = _matmul(hp, w_kv[None], 0, bm=bm, bn=tile(w_kv.shape[1]), out_dtype=F32,
                                 side=(hs, (), (), F32), name="kv")
    y_p = _final_norm(xp, norm_final_g[None], bm=bmn)
    y_s = _final_norm(xs, norm_final_g[None], bm=ms)
    return (y_p, qkv_p, kv_p, tails_p), (y_s, qkv_s, kv_s, tails_s)


def kernel(x_prompt, x_sample, cache_win0, cache_win1, cache_win2, cache_kv_pages, state_conv, page_table,
           c_prompt, c_sample, norm_attn_g, norm_ffn_g, w_ada, b_ada, w_qkv_a, w_o_a, w_q_b, w_o_b, sb_bias,
           norm_kv_g, w_ada_kv, b_ada_kv, w_kv, w_up, w_conv, b_conv, w_down, norm_final_g):
    batch, seq, d = x_prompt.shape
    db, dec_seq, _ = x_sample.shape
    depth = w_up.shape[0]
    n_a = w_qkv_a.shape[0]
    heads_a = w_o_a.shape[1] // HEAD_DIM
    heads_b = w_q_b.shape[2] // HEAD_DIM
    d_ff = w_down.shape[1]
    page = cache_kv_pages.shape[1]
    past_len = page_table.shape[1] * page
    caches = (cache_win0, cache_win1, cache_win2)
    assert seq % (BAND * DILATIONS[-1]) == 0 and batch + db <= ADA_ROWS and db == SUBLANES
    assert all(caches[g].shape[2] == WINDOWS[g] for g in range(N_GROUPS_A)) and dec_seq <= DILATIONS[1]

    c_all = jnp.concatenate([c_prompt, c_sample, jnp.zeros((ADA_ROWS - batch - db, d), F32)], axis=0)

    def ada(w, b, layer):
        n = w.shape[2]
        bn = _pick(n, (1024, 512, 256, 128))
        return _matmul(c_all, w, layer, bm=ADA_ROWS, bn=bn, out_dtype=F32, mode="bias", silu_in=True,
                       extras=(b.reshape(w.shape[0], 1, n),),
                       extra_specs=(pl.BlockSpec((None, 1, bn), lambda j, i: (layer, 0, j)),), name="ada")

    mod_all = [ada(w_ada, b_ada, l) for l in range(depth)]
    mod_kv_all = ada(w_ada_kv[None], b_ada_kv[None], 0)

    def prompt_mods(a, chunks):
        return tuple(c.reshape(batch, 1, d) for c in jnp.split(a[:batch], chunks, axis=-1))

    def sample_mods(a, chunks):
        return tuple(jnp.tile(c, (dec_seq, 1))[None] for c in jnp.split(a[batch:batch + db], chunks, axis=-1))

    b_conv3 = b_conv.reshape(depth, 1, 2 * d_ff)
    weights = (norm_attn_g, norm_ffn_g, w_qkv_a, w_o_a, w_q_b, w_o_b, sb_bias, norm_kv_g, w_kv,
               w_up, w_conv, b_conv3, w_down, norm_final_g)

    m_s = dec_seq * db
    x_s = x_sample.transpose(1, 0, 2).reshape(m_s, d)
    pages = cache_kv_pages.reshape(cache_kv_pages.shape[0], page * 2 * heads_b, HEAD_DIM)
    prev = state_conv.transpose(0, 2, 1, 3).reshape(depth, (CONV_W - 1) * db, 2 * d_ff)

    def n_major(a):
        return a.reshape(dec_seq, db, a.shape[-1]).transpose(1, 0, 2)

    def t_major(a):
        return a.transpose(1, 0, 2).reshape(m_s, a.shape[-1])

    def attn_b_sample(q, kv, bias):
        kv_new = n_major(kv).reshape(db, dec_seq * 2 * heads_b, HEAD_DIM)
        kv_pad = jnp.concatenate([kv_new, jnp.zeros((db, (page - dec_seq) * 2 * heads_b, HEAD_DIM), F32)], axis=1)
        return t_major(_sb_sample(n_major(q), kv_pad, pages, page_table, bias, heads=heads_b, page=page))

    bm_p = _pick(seq, (1024, 512, 256, 128))
    sb_blk = _pick(seq, (256, 128))
    cfg = dict(
        seq=seq, bm=bm_p, bm_norm=_pick(seq, (256, 128)), bm_down=_pick(seq, (2048, 1024, 512, 256, 128)),
        bn_down=1024, tap_s=db,
        tab_map=lambda j, i: (i % (seq // bm_p), 0),
        attn_a_p=lambda qkv: _attn_a_prompt(qkv, batch=batch, seq=seq, heads=heads_a),
        attn_b_p=lambda q, kv, bias: _sb_prompt(q, kv, bias, batch=batch, seq=seq, heads=heads_b, blk=sb_blk),
        attn_a_s=lambda qkv, l: t_major(_attn_a_sample(n_major(qkv), caches, l, heads=heads_a)),
        attn_b_s=attn_b_sample,
    )
    pos_s = past_len + jnp.repeat(jnp.arange(dec_seq), db)
    (y_p, qkv_p, kv_p, tails_p), (y_s, qkv_s, kv_s, tails_s) = _trunk(
        x_prompt.reshape(batch * seq, d), x_s,
        [prompt_mods(mod_all[l], 6) for l in range(depth)], [sample_mods(mod_all[l], 6) for l in range(depth)],
        prompt_mods(mod_kv_all, 2), sample_mods(mod_kv_all, 2),
        _rope_table(jnp.arange(seq)), _rope_table(pos_s), prev, cfg, weights)

    win_p = []
    for g in range(N_GROUPS_A):
        keep = min(WINDOWS[g], seq)
        per_layer = [q.reshape(batch, seq, N_GROUPS_A, 3, heads_a, HEAD_DIM)[:, seq - keep:, g, 1:3]
                     for q in qkv_p]
        win_p.append(jnp.stack(per_layer, axis=0))
    kv_prompt = kv_p.reshape(batch, seq, 2, heads_b, HEAD_DIM)
    nblk = seq // bm_p
    conv_p = jnp.stack([
        jnp.concatenate([tg.reshape(batch, nblk, SUBLANES, d_ff)[:, -1, SUBLANES - (CONV_W - 1):],
                         tu.reshape(batch, nblk, SUBLANES, d_ff)[:, -1, SUBLANES - (CONV_W - 1):]], axis=-1)
        for tg, tu in tails_p], axis=0)

    qkv_new = jnp.stack([n_major(q) for q in qkv_s], axis=0).reshape(
        n_a, db, dec_seq, N_GROUPS_A, 3, heads_a, HEAD_DIM)
    win_s = []
    for g in range(N_GROUPS_A):
        c = caches[g]
        pads = [(0, 0, 0)] * c.ndim
        pads[2] = (-dec_seq, dec_seq, 0)
        win_s.append(lax.dynamic_update_slice(lax.pad(c, jnp.zeros((), c.dtype), pads), qkv_new[:, :, :, g, 1:3],
                                              (0, 0, c.shape[2] - dec_seq, 0, 0, 0)))
    kv_sample = n_major(kv_s).reshape(db, dec_seq, 2, heads_b, HEAD_DIM)
    conv_s = jnp.stack([
        jnp.concatenate([tg, tu], axis=-1).reshape(CONV_W - 1, db, 2 * d_ff).transpose(1, 0, 2)
        for tg, tu in tails_s], axis=0)

    y_prompt = y_p.reshape(batch, seq, d)
    y_sample = y_s.reshape(dec_seq, db, d).transpose(1, 0, 2)
    return (y_prompt, y_sample, win_p[0], win_s[0], win_p[1], win_s[1], win_p[2], win_s[2],
            kv_prompt, kv_sample, conv_p, conv_s)
```

```python
import functools

import jax
import jax.numpy as jnp
from jax import lax
from jax.experimental import pallas as pl
from jax.experimental.pallas import tpu as pltpu

F32 = jnp.float32
BF16 = jnp.bfloat16

HEAD_DIM = 128
N_GROUPS_A = 3
WINDOWS = (128, 512, 2048)
DILATIONS = (1, 4, 16)
ROT_DIM = HEAD_DIM // 4
ROPE_THETA = 500000.0
CONV_W = 3
RMS_EPS = 1e-6
NEG_BIG = -1e30
UNROLL_A = 8
K_BLOCK_BIG = 512
BAND = 128
SUBLANES = 8
ADA_ROWS = 16
VMEM_BIG = 56 * 1024 * 1024
VMEM_MID = 48 * 1024 * 1024
VMEM_DOWN = 44 * 1024 * 1024


def _params(sem, vmem=VMEM_MID):
    return pltpu.CompilerParams(dimension_semantics=sem, vmem_limit_bytes=vmem)


def _sigmoid(x):
    return 1.0 / (1.0 + jnp.exp(-x))


def _norm_kernel(x_ref, g_ref, sc_ref, sh_ref, o_ref):
    x = x_ref[...]
    ms = jnp.mean(x * x, axis=-1, keepdims=True)
    y = x * lax.rsqrt(ms + RMS_EPS) * g_ref[...]
    o_ref[...] = (y * (1.0 + sc_ref[...]) + sh_ref[...]).astype(o_ref.dtype)


def _final_norm_kernel(x_ref, g_ref, o_ref):
    x = x_ref[...]
    ms = jnp.mean(x * x, axis=-1, keepdims=True)
    o_ref[...] = (x * lax.rsqrt(ms + RMS_EPS) * g_ref[...]).astype(o_ref.dtype)


def _norm_mod(x, g, sc, sh, *, bm, rows_per_mod):
    m, d = x.shape
    r = sc.shape[1]
    mod_spec = pl.BlockSpec((None, r, d), lambda i: ((i * bm) // rows_per_mod, 0, 0))
    return pl.pallas_call(
        _norm_kernel,
        grid=(m // bm,),
        in_specs=[pl.BlockSpec((bm, d), lambda i: (i, 0)),
                  pl.BlockSpec((1, d), lambda i: (0, 0)),
                  mod_spec, mod_spec],
        out_specs=pl.BlockSpec((bm, d), lambda i: (i, 0)),
        out_shape=jax.ShapeDtypeStruct((m, d), BF16),
        compiler_params=_params(("arbitrary",)),
        name="norm_mod",
    )(x, g, sc, sh)


def _final_norm(x, g, *, bm):
    m, d = x.shape
    return pl.pallas_call(
        _final_norm_kernel,
        grid=(m // bm,),
        in_specs=[pl.BlockSpec((bm, d), lambda i: (i, 0)),
                  pl.BlockSpec((1, d), lambda i: (0, 0))],
        out_specs=pl.BlockSpec((bm, d), lambda i: (i, 0)),
        out_shape=jax.ShapeDtypeStruct((m, d), F32),
        compiler_params=_params(("arbitrary",)),
        name="final_norm",
    )(x, g)


def _cast_weight(w_ref, wb_ref):
    k = w_ref.shape[0]
    chunk = 512 if k % 512 == 0 else k

    def body(c, _):
        r0 = pl.multiple_of(c * chunk, chunk)
        wb_ref[pl.ds(r0, chunk), :] = w_ref[pl.ds(r0, chunk), :].astype(BF16)
        return 0

    lax.fori_loop(0, k // chunk, body, 0)


def _rope_tile(acc, tab_ref):
    cos = tab_ref[:, 0:HEAD_DIM]
    sin = tab_ref[:, HEAD_DIM:2 * HEAD_DIM]
    half = ROT_DIM // 2
    lane = lax.broadcasted_iota(jnp.int32, (acc.shape[0], HEAD_DIM), 1)
    partner = jnp.where(lane < half, lane + half, jnp.where(lane < ROT_DIM, lane - half, lane))
    pieces = []
    for hh in range(acc.shape[1] // HEAD_DIM):
        x = acc[:, hh * HEAD_DIM:(hh + 1) * HEAD_DIM]
        pieces.append(x * cos + jnp.take_along_axis(x, partner, axis=1) * sin)
    return pieces


_N_EXTRA = {"plain": 0, "bias": 1, "resid": 2, "rope": 1}


def _mm_epilogue(acc, extra, o_ref, *, mode, qkv_width, out_scale):
    if mode == "plain":
        if out_scale != 1.0:
            acc = acc * out_scale
        o_ref[...] = acc.astype(o_ref.dtype)
    elif mode == "bias":
        o_ref[...] = (acc + extra[0][...]).astype(o_ref.dtype)
    elif mode == "resid":
        res_ref, gate_ref = extra
        o_ref[...] = (res_ref[...] + gate_ref[...] * acc).astype(o_ref.dtype)
    else:
        bn = acc.shape[1]
        kind = ((pl.program_id(0) * bn) // qkv_width) % 3

        @pl.when(kind == 2)
        def _():
            o_ref[...] = acc.astype(o_ref.dtype)

        @pl.when(kind != 2)
        def _():
            for hh, piece in enumerate(_rope_tile(acc, extra[0])):
                o_ref[:, hh * HEAD_DIM:(hh + 1) * HEAD_DIM] = piece.astype(o_ref.dtype)


def _mm_kernel(*refs, mode, silu_in, qkv_width, out_scale, side):
    ne = _N_EXTRA[mode]
    x_ref, w_ref = refs[0], refs[1]
    extra = refs[2:2 + ne]
    pos = 2 + ne
    if side:
        xs_ref, extra_s = refs[pos], refs[pos + 1:pos + 1 + ne]
        pos += 1 + ne
    o_ref = refs[pos]
    os_ref = refs[pos + 1] if side else None
    wb_ref = refs[-1]
    epilogue = functools.partial(_mm_epilogue, mode=mode, qkv_width=qkv_width, out_scale=out_scale)

    @pl.when(pl.program_id(1) == 0)
    def _():
        _cast_weight(w_ref, wb_ref)
        if side:
            acc_s = jnp.dot(xs_ref[...].astype(BF16), wb_ref[...], preferred_element_type=F32)
            epilogue(acc_s, extra_s, os_ref)

    x = x_ref[...]
    if silu_in:
        x = x * _sigmoid(x)
    epilogue(jnp.dot(x.astype(BF16), wb_ref[...], preferred_element_type=F32), extra, o_ref)


def _matmul(x, w, layer, *, bm, bn, out_dtype, mode="plain", extras=(), extra_specs=(),
            silu_in=False, qkv_width=0, out_scale=1.0, side=None, vmem=VMEM_BIG, name="matmul"):
    m, k = x.shape
    n = w.shape[2]
    kern = functools.partial(_mm_kernel, mode=mode, silu_in=silu_in, qkv_width=qkv_width, out_scale=out_scale,
                             side=side is not None)
    in_specs = [pl.BlockSpec((bm, k), lambda j, i: (i, 0)),
                pl.BlockSpec((None, k, bn), lambda j, i: (layer, 0, j)),
                *extra_specs]
    args = [x, w, *extras]
    out_specs = pl.BlockSpec((bm, bn), lambda j, i: (i, j))
    out_shape = jax.ShapeDtypeStruct((m, n), out_dtype)
    if side is not None:
        xs, extras_s, specs_s, dtype_s = side
        ms = xs.shape[0]
        in_specs += [pl.BlockSpec((ms, k), lambda j, i: (0, 0)), *specs_s]
        args += [xs, *extras_s]
        out_specs = [out_specs, pl.BlockSpec((ms, bn), lambda j, i: (0, j))]
        out_shape = [out_shape, jax.ShapeDtypeStruct((ms, n), dtype_s)]
    return pl.pallas_call(
        kern,
        grid=(n // bn, m // bm),
        in_specs=in_specs,
        out_specs=out_specs,
        out_shape=out_shape,
        scratch_shapes=[pltpu.VMEM((k, bn), BF16)],
        compiler_params=_params(("arbitrary", "arbitrary"), vmem),
        name=name,
    )(*args)


def _mm_ktiled_kernel(*refs, nk_big, nk_small, side):
    nw = 2 if nk_small else 1
    per = nw + 2
    w_refs = refs[:nw]
    groups = [refs[nw:nw + per]] + ([refs[nw + per:nw + 2 * per]] if side else [])
    outs = refs[nw + len(groups) * per:]
    kk = pl.program_id(2)
    first_rows = pl.program_id(1) == 0

    def each_group(fn):
        fn(groups[0], outs[0])
        if side:
            @pl.when(first_rows)
            def _():
                fn(groups[1], outs[1])

    @pl.when(kk == 0)
    def _():
        wb = w_refs[0][...].astype(BF16)

        def start(g, o_ref):
            o_ref[...] = jnp.dot(g[0][...], wb, preferred_element_type=F32)
        each_group(start)

    @pl.when(jnp.logical_and(kk > 0, kk < nk_big))
    def _():
        wb = w_refs[0][...].astype(BF16)

        def add(g, o_ref):
            o_ref[...] += jnp.dot(g[0][...], wb, preferred_element_type=F32)
        each_group(add)

    if nk_small:
        @pl.when(kk >= nk_big)
        def _():
            ws = w_refs[1][...].astype(BF16)

            def add_small(g, o_ref):
                o_ref[...] += jnp.dot(g[1][...], ws, preferred_element_type=F32)
            each_group(add_small)

    @pl.when(kk == nk_big + nk_small - 1)
    def _():
        def finish(g, o_ref):
            o_ref[...] = g[nw][...] + g[nw + 1][...] * o_ref[...]
        each_group(finish)


def _matmul_ktiled_resid(x, w, layer, res, gate, *, bm, bn, rows_per_mod, side=None, name="down"):
    m, k = x.shape
    n = w.shape[2]
    r = gate.shape[1]
    bk_big = min(k, K_BLOCK_BIG)
    nk_big = k // bk_big
    rem = k - nk_big * bk_big
    bk_small = _pick(rem, (256, 128)) if rem else 0
    nk_small = rem // bk_small if rem else 0
    small0 = (nk_big * bk_big) // bk_small if rem else 0
    assert rem == 0 or (nk_big * bk_big) % bk_small == 0

    def big(kk):
        return jnp.minimum(kk, nk_big - 1)

    def small(kk):
        return small0 + jnp.maximum(kk - nk_big, 0)

    in_specs = [pl.BlockSpec((None, bk_big, bn), lambda j, i, kk: (layer, big(kk), j))]
    args = [w]
    if nk_small:
        in_specs.append(pl.BlockSpec((None, bk_small, bn), lambda j, i, kk: (layer, small(kk), j)))
        args.append(w)
    in_specs.append(pl.BlockSpec((bm, bk_big), lambda j, i, kk: (i, big(kk))))
    args.append(x)
    if nk_small:
        in_specs.append(pl.BlockSpec((bm, bk_small), lambda j, i, kk: (i, small(kk))))
        args.append(x)
    in_specs += [pl.BlockSpec((bm, bn), lambda j, i, kk: (i, j), pipeline_mode=pl.Buffered(1)),
                 pl.BlockSpec((None, r, bn), lambda j, i, kk: ((i * bm) // rows_per_mod, 0, j))]
    args += [res, gate]
    out_specs = pl.BlockSpec((bm, bn), lambda j, i, kk: (i, j))
    out_shape = jax.ShapeDtypeStruct((m, n), F32)
    if side is not None:
        xs, res_s, gate_s = side
        ms = xs.shape[0]
        in_specs.append(pl.BlockSpec((ms, bk_big), lambda j, i, kk: (0, big(kk))))
        args.append(xs)
        if nk_small:
            in_specs.append(pl.BlockSpec((ms, bk_small), lambda j, i, kk: (0, small(kk))))
            args.append(xs)
        in_specs += [pl.BlockSpec((ms, bn), lambda j, i, kk: (0, j)),
                     pl.BlockSpec((None, ms, bn), lambda j, i, kk: (0, 0, j))]
        args += [res_s, gate_s]
        out_specs = [out_specs, pl.BlockSpec((ms, bn), lambda j, i, kk: (0, j))]
        out_shape = [out_shape, jax.ShapeDtypeStruct((ms, n), F32)]
    return pl.pallas_call(
        functools.partial(_mm_ktiled_kernel, nk_big=nk_big, nk_small=nk_small, side=side is not None),
        grid=(n // bn, m // bm, nk_big + nk_small),
        in_specs=in_specs,
        out_specs=out_specs,
        out_shape=out_shape,
        compiler_params=_params(("arbitrary", "arbitrary", "arbitrary"), VMEM_DOWN),
        name=name,
    )(*args)


def _ffn_rows(h_ref, act_ref, stg_ref, stu_ref, ug, uu, wgb, wub, conv_refs, *, rows, halo, tap):
    cwg_ref, cwu_ref, cbg_ref, cbu_ref = conv_refs
    h = h_ref[...]
    ug[halo:halo + rows, :] = jnp.dot(h, wgb[...], preferred_element_type=F32)
    uu[halo:halo + rows, :] = jnp.dot(h, wub[...], preferred_element_type=F32)

    def conv(u, cw_ref, cb_ref):
        y = cb_ref[...] + cw_ref[0:1, :] * u[halo - 2 * tap:halo - 2 * tap + rows, :]
        y = y + cw_ref[1:2, :] * u[halo - tap:halo - tap + rows, :]
        return y + cw_ref[2:3, :] * u[halo:halo + rows, :]

    gate = conv(ug, cwg_ref, cbg_ref)
    up = conv(uu, cwu_ref, cbu_ref)
    act_ref[...] = (gate * _sigmoid(gate) * up).astype(act_ref.dtype)
    tail_g = ug[rows:rows + halo, :]
    tail_u = uu[rows:rows + halo, :]
    stg_ref[...] = tail_g
    stu_ref[...] = tail_u
    ug[0:halo, :] = tail_g
    uu[0:halo, :] = tail_u


def _ffn_up_kernel(*refs, bm, halo, blocks_per_seq, side):
    h_ref, wg_ref, wu_ref = refs[0:3]
    conv_refs = refs[3:7]
    pos = 7
    if side:
        hs_ref, pgs_ref, pus_ref = refs[pos:pos + 3]
        pos += 3
    act_ref, stg_ref, stu_ref = refs[pos:pos + 3]
    pos += 3
    if side:
        acts_ref, stgs_ref, stus_ref = refs[pos:pos + 3]
        pos += 3
    wgb, wub, ug, uu = refs[pos:pos + 4]
    i = pl.program_id(1)

    @pl.when(i == 0)
    def _():
        _cast_weight(wg_ref, wgb)
        _cast_weight(wu_ref, wub)
        if side:
            halo_s, tap_s = side
            ugs, uus = refs[pos + 4:pos + 6]
            ugs[0:halo_s, :] = pgs_ref[...]
            uus[0:halo_s, :] = pus_ref[...]
            _ffn_rows(hs_ref, acts_ref, stgs_ref, stus_ref, ugs, uus, wgb, wub, conv_refs,
                      rows=hs_ref.shape[0], halo=halo_s, tap=tap_s)

    @pl.when(i % blocks_per_seq == 0)
    def _():
        ug[0:halo, :] = jnp.zeros((halo, ug.shape[1]), F32)
        uu[0:halo, :] = jnp.zeros((halo, uu.shape[1]), F32)

    _ffn_rows(h_ref, act_ref, stg_ref, stu_ref, ug, uu, wgb, wub, conv_refs, rows=bm, halo=halo, tap=1)


def _ffn_up(h, w_up, w_conv, b_conv3, layer, *, bm, bn, rows_per_seq, side=None):
    m, d = h.shape
    d_ff = w_up.shape[2] // 2
    nj = d_ff // bn
    halo = SUBLANES
    in_specs = [pl.BlockSpec((bm, d), lambda j, i: (i, 0)),
                pl.BlockSpec((None, d, bn), lambda j, i: (layer, 0, j)),
                pl.BlockSpec((None, d, bn), lambda j, i: (layer, 0, j + nj)),
                pl.BlockSpec((None, CONV_W, bn), lambda j, i: (layer, 0, j)),
                pl.BlockSpec((None, CONV_W, bn), lambda j, i: (layer, 0, j + nj)),
                pl.BlockSpec((None, 1, bn), lambda j, i: (layer, 0, j)),
                pl.BlockSpec((None, 1, bn), lambda j, i: (layer, 0, j + nj))]
    args = [h, w_up, w_up, w_conv, w_conv, b_conv3, b_conv3]
    st_spec = pl.BlockSpec((None, halo, bn), lambda j, i: (i, 0, j))
    out_specs = [pl.BlockSpec((bm, bn), lambda j, i: (i, j)), st_spec, st_spec]
    out_shape = [jax.ShapeDtypeStruct((m, d_ff), BF16),
                 jax.ShapeDtypeStruct((m // bm, halo, d_ff), F32),
                 jax.ShapeDtypeStruct((m // bm, halo, d_ff), F32)]
    scratch = [pltpu.VMEM((d, bn), BF16), pltpu.VMEM((d, bn), BF16),
               pltpu.VMEM((bm + halo, bn), F32), pltpu.VMEM((bm + halo, bn), F32)]
    side_cfg = None
    if side is not None:
        hs, prev, tap_s = side
        ms = hs.shape[0]
        halo_s = (CONV_W - 1) * tap_s
        side_cfg = (halo_s, tap_s)
        in_specs += [pl.BlockSpec((ms, d), lambda j, i: (0, 0)),
                     pl.BlockSpec((halo_s, bn), lambda j, i: (0, j)),
                     pl.BlockSpec((halo_s, bn), lambda j, i: (0, j + nj))]
        args += [hs, prev, prev]
        sts_spec = pl.BlockSpec((halo_s, bn), lambda j, i: (0, j))
        out_specs += [pl.BlockSpec((ms, bn), lambda j, i: (0, j)), sts_spec, sts_spec]
        out_shape += [jax.ShapeDtypeStruct((ms, d_ff), BF16),
                      jax.ShapeDtypeStruct((halo_s, d_ff), F32), jax.ShapeDtypeStruct((halo_s, d_ff), F32)]
        scratch += [pltpu.VMEM((ms + halo_s, bn), F32), pltpu.VMEM((ms + halo_s, bn), F32)]
    kern = functools.partial(_ffn_up_kernel, bm=bm, halo=halo, blocks_per_seq=rows_per_seq // bm, side=side_cfg)
    return pl.pallas_call(
        kern,
        grid=(nj, m // bm),
        in_specs=in_specs,
        out_specs=out_specs,
        out_shape=out_shape,
        scratch_shapes=scratch,
        compiler_params=_params(("arbitrary", "arbitrary"), VMEM_BIG),
        name="ffn_up",
    )(*args)


def _attn_a_prompt_kernel(q0, k0, v0, q1, k1, v1, q2, k2, v2, o_ref, og, lg, *, seq):
    qkv = ((q0, k0, v0), (q1, k1, v1), (q2, k2, v2))
    scale = HEAD_DIM ** -0.5
    row = lax.broadcasted_iota(jnp.int32, (BAND, 2 * BAND), 0)
    col = lax.broadcasted_iota(jnp.int32, (BAND, 2 * BAND), 1)
    cur_ok = jnp.logical_and(col >= BAND, col - BAND <= row)
    prev_ok = jnp.logical_and(col < BAND, col >= row)
    dn = (((1,), (1,)), ((), ()))

    for g in range(N_GROUPS_A):
        dil = DILATIONS[g]
        nb = seq // dil // BAND
        q_ref, k_ref, v_ref = qkv[g]
        unroll = UNROLL_A if (dil * nb) % UNROLL_A == 0 else 1

        def rows(start, dil=dil):
            if dil == 1:
                return pl.ds(pl.multiple_of(start, BAND), BAND)
            return pl.ds(start, BAND, stride=dil)

        def body(it, _, g=g, dil=dil, nb=nb, q_ref=q_ref, k_ref=k_ref, v_ref=v_ref, rows=rows, unroll=unroll):
            starts, oks, scores, vbands = [], [], [], []
            for u in range(unroll):
                idx = it * unroll + u
                blk = idx % nb
                start = idx // nb + blk * (BAND * dil)
                has_prev = blk > 0
                pstart = jnp.where(has_prev, start - BAND * dil, start)
                q = q_ref[rows(start), :].astype(BF16)
                kband = jnp.concatenate([k_ref[rows(pstart), :], k_ref[rows(start), :]], axis=0).astype(BF16)
                vbands.append(jnp.concatenate([v_ref[rows(pstart), :], v_ref[rows(start), :]], axis=0).astype(BF16))
                scores.append(lax.dot_general(q, kband, dn, preferred_element_type=F32))
                starts.append(start)
                oks.append(jnp.logical_or(cur_ok, jnp.logical_and(prev_ok, has_prev)))
            probs, dens, lses = [], [], []
            for u in range(unroll):
                sc = jnp.where(oks[u], scores[u] * scale, NEG_BIG)
                mx = jnp.max(sc, axis=1, keepdims=True)
                p = jnp.exp(sc - mx)
                den = jnp.sum(p, axis=1, keepdims=True)
                probs.append(p.astype(BF16))
                dens.append(den)
                lses.append(mx + jnp.log(den))
            for u in range(unroll):
                o = jnp.dot(probs[u], vbands[u], preferred_element_type=F32)
                og[g, rows(starts[u]), :] = o / dens[u]
                lg[g, rows(starts[u]), :] = jnp.broadcast_to(lses[u], (BAND, HEAD_DIM))
            return 0

        lax.fori_loop(0, dil * nb // unroll, body, 0)

    chunk = 256

    def comb(c, _):
        r0 = pl.multiple_of(c * chunk, chunk)
        l0 = lg[0, pl.ds(r0, chunk), :]
        l1 = lg[1, pl.ds(r0, chunk), :]
        l2 = lg[2, pl.ds(r0, chunk), :]
        mx = jnp.maximum(jnp.maximum(l0, l1), l2)
        e0 = jnp.exp(l0 - mx)
        e1 = jnp.exp(l1 - mx)
        e2 = jnp.exp(l2 - mx)
        tot = e0 + e1 + e2
        o = (og[0, pl.ds(r0, chunk), :] * (e0 / tot) + og[1, pl.ds(r0, chunk), :] * (e1 / tot)
             + og[2, pl.ds(r0, chunk), :] * (e2 / tot))
        o_ref[pl.ds(r0, chunk), :] = o.astype(o_ref.dtype)
        return 0

    lax.fori_loop(0, seq // chunk, comb, 0)


def _attn_a_prompt(qkv, *, batch, seq, heads):
    m = qkv.shape[0]
    specs = []
    for g in range(N_GROUPS_A):
        for t in range(3):
            specs.append(pl.BlockSpec((seq, HEAD_DIM),
                                      lambda n, h, g=g, t=t: (n, (g * 3 + t) * heads + h)))
    return pl.pallas_call(
        functools.partial(_attn_a_prompt_kernel, seq=seq),
        grid=(batch, heads),
        in_specs=specs,
        out_specs=pl.BlockSpec((seq, HEAD_DIM), lambda n, h: (n, h)),
        out_shape=jax.ShapeDtypeStruct((m, heads * HEAD_DIM), BF16),
        scratch_shapes=[pltpu.VMEM((N_GROUPS_A, seq, HEAD_DIM), F32),
                        pltpu.VMEM((N_GROUPS_A, seq, HEAD_DIM), F32)],
        compiler_params=_params(("arbitrary", "arbitrary")),
        name="attn_a_prompt",
    )(*([qkv] * 9))


def _attn_a_sample_kernel(qn_ref, c0_ref, c1_ref, c2_ref, o_ref, *, heads, dec_seq):
    width = heads * HEAD_DIM
    scale = HEAD_DIM ** -0.5
    caches = (c0_ref, c1_ref, c2_ref)
    hrow = lax.broadcasted_iota(jnp.int32, (heads, width), 0)
    hcol = lax.broadcasted_iota(jnp.int32, (heads, width), 1) // HEAD_DIM
    head_mask = hrow == hcol
    key_idx = lax.broadcasted_iota(jnp.int32, (heads, BAND), 1)
    dn = (((1,), (1,)), ((), ()))

    def split_heads(x):
        y = pltpu.einshape("mrd->rmd", x)
        k = jnp.concatenate([y[h] for h in range(heads)], axis=1).astype(BF16)
        v = jnp.concatenate([y[heads + h] for h in range(heads)], axis=1).astype(BF16)
        return k, v

    rh = 2 * heads
    kv_bufs = [[split_heads(caches[0][...])]]
    for g in range(1, N_GROUPS_A):
        kv_bufs.append([split_heads(caches[g][:, t * rh:(t + 1) * rh, :]) for t in range(dec_seq)])

    for t in range(dec_seq):
        outs, lses = [], []
        for g in range(N_GROUPS_A):
            base = g * 3 * width
            q_row = qn_ref[t:t + 1, base:base + width]
            qbd = jnp.where(head_mask, jnp.broadcast_to(q_row, (heads, width)), 0.0)
            kb, vb = kv_bufs[g][0 if g == 0 else t]
            s_buf = lax.dot_general(qbd.astype(BF16), kb, dn, preferred_element_type=F32) * scale
            if g == 0:
                s_buf = jnp.where(key_idx >= t, s_buf, NEG_BIG)
                new_rows = range(t + 1)
            else:
                new_rows = (t,)
            s_new = []
            for tn in new_rows:
                k_row = qn_ref[tn:tn + 1, base + width:base + 2 * width]
                s_new.append(jnp.sum(qbd * k_row, axis=1, keepdims=True) * scale)
            mx = jnp.max(s_buf, axis=1, keepdims=True)
            for s in s_new:
                mx = jnp.maximum(mx, s)
            p_buf = jnp.exp(s_buf - mx)
            den = jnp.sum(p_buf, axis=1, keepdims=True)
            o = jnp.dot(p_buf.astype(BF16), vb, preferred_element_type=F32)
            for tn, s in zip(new_rows, s_new):
                p = jnp.exp(s - mx)
                den = den + p
                o = o + p * qn_ref[tn:tn + 1, base + 2 * width:base + 3 * width]
            o = jnp.where(head_mask, o / den, 0.0)
            lse = jnp.where(head_mask, jnp.broadcast_to(mx + jnp.log(den), (heads, width)), 0.0)
            outs.append(jnp.sum(o, axis=0, keepdims=True))
            lses.append(jnp.sum(lse, axis=0, keepdims=True))
        mx = jnp.maximum(jnp.maximum(lses[0], lses[1]), lses[2])
        es = [jnp.exp(l - mx) for l in lses]
        tot = es[0] + es[1] + es[2]
        row = outs[0] * (es[0] / tot) + outs[1] * (es[1] / tot) + outs[2] * (es[2] / tot)
        o_ref[t:t + 1, :] = row.astype(o_ref.dtype)


def _attn_a_sample(qn, caches, layer, *, heads):
    db, dec_seq, _ = qn.shape
    width = heads * HEAD_DIM
    views, specs = [], []
    rh = 2 * heads
    for g in range(N_GROUPS_A):
        c = caches[g]
        nlay, _, nbuf = c.shape[:3]
        dil = DILATIONS[g]
        views.append(c.reshape(nlay, db, nbuf // dil, dil * rh, HEAD_DIM))
        need = rh if g == 0 else dec_seq * rh
        blk_r = need if need == dil * rh else -(-need // SUBLANES) * SUBLANES
        specs.append(pl.BlockSpec((None, None, nbuf // dil, blk_r, HEAD_DIM), lambda n: (layer, n, 0, 0, 0)))
    return pl.pallas_call(
        functools.partial(_attn_a_sample_kernel, heads=heads, dec_seq=dec_seq),
        grid=(db,),
        in_specs=[pl.BlockSpec((None, dec_seq, qn.shape[2]), lambda n: (n, 0, 0)), *specs],
        out_specs=pl.BlockSpec((None, dec_seq, width), lambda n: (n, 0, 0)),
        out_shape=jax.ShapeDtypeStruct((db, dec_seq, width), F32),
        compiler_params=_params(("arbitrary",)),
        name="attn_a_sample",
    )(qn, *views)


LOG2E = 1.4426950408889634
LN2 = 0.6931471805599453
SB_QSCALE = HEAD_DIM ** -0.5 * LOG2E


def _sb_blocks(x2s, tri, masks, vs):
    sps = []
    for x2, mask in zip(x2s, masks):
        neg_abs = pltpu.bitcast(pltpu.bitcast(x2, jnp.uint32) | jnp.uint32(0x80000000), F32)
        sp = LN2 * jnp.maximum(x2, 0.0) + jnp.log(1.0 + jnp.exp2(neg_abs))
        sps.append(sp if mask is None else jnp.where(mask, sp, 0.0))
    incls = [jnp.dot(sp.astype(BF16), tri, preferred_element_type=F32) for sp in sps]
    ws = []
    for x2, incl, mask in zip(x2s, incls, masks):
        a = jnp.exp2(x2 - LOG2E * incl)
        ws.append((a if mask is None else jnp.where(mask, a, 0.0)).astype(BF16))
    pvs = [jnp.dot(a, v, preferred_element_type=F32) for a, v in zip(ws, vs)]
    return [(pv, jnp.sum(sp, axis=1, keepdims=True)) for pv, sp in zip(pvs, sps)]


def _sb_prompt_kernel(q_ref, k_ref, v_ref, b_ref, tri_ref, o_ref, kb_ref, vb_ref, *, seq, blk, hpb):
    nblk = seq // blk
    chunk = 512 if seq % 512 == 0 else seq

    def cast(c, _):
        r0 = pl.multiple_of(c * chunk, chunk)
        kb_ref[pl.ds(r0, chunk), :] = k_ref[pl.ds(r0, chunk), :].astype(BF16)
        vb_ref[pl.ds(r0, chunk), :] = v_ref[pl.ds(r0, chunk), :].astype(BF16)
        return 0

    lax.fori_loop(0, seq // chunk, cast, 0)
    row = lax.broadcasted_iota(jnp.int32, (blk, blk), 0)
    col = lax.broadcasted_iota(jnp.int32, (blk, blk), 1)
    diag_mask = col < row
    dn = (((1,), (1,)), ((), ()))
    lanes = [slice(hh * HEAD_DIM, (hh + 1) * HEAD_DIM) for hh in range(hpb)]

    def q_block(qb, _):
        q0 = pl.multiple_of(qb * blk, blk)
        qs = [q_ref[pl.ds(q0, blk), ln] for ln in lanes]

        def blocks(kbs, mask):
            k0s = [pl.multiple_of(kb * blk, blk) for kb in kbs]
            x2s = [lax.dot_general(qs[hh], kb_ref[pl.ds(k0, blk), lanes[hh]], dn,
                                   preferred_element_type=F32) + b_ref[hh]
                   for hh in range(hpb) for k0 in k0s]
            vs = [vb_ref[pl.ds(k0, blk), lanes[hh]] for hh in range(hpb) for k0 in k0s]
            out = _sb_blocks(x2s, tri_ref[...], [mask] * len(x2s), vs)
            return [out[hh * len(kbs):(hh + 1) * len(kbs)] for hh in range(hpb)]

        state = tuple(res[0] for res in blocks([qb], diag_mask))

        def pair(it, state):
            res = blocks([qb - 1 - 2 * it, qb - 2 - 2 * it], None)
            out = []
            for (acc, carry), ((pv0, m0), (pv1, m1)) in zip(state, res):
                acc = acc + jnp.exp(-carry) * pv0 + jnp.exp(-(carry + m0)) * pv1
                out.append((acc, carry + m0 + m1))
            return tuple(out)

        state = lax.fori_loop(0, qb // 2, pair, state)

        def last(state):
            res = blocks([0], None)
            return tuple((acc + jnp.exp(-carry) * pv, carry + m)
                         for (acc, carry), ((pv, m),) in zip(state, res))

        state = lax.cond(qb % 2 == 1, last, lambda st: st, state)
        for hh, (acc, _) in enumerate(state):
            o_ref[pl.ds(q0, blk), lanes[hh]] = acc.astype(o_ref.dtype)
        return 0

    lax.fori_loop(0, nblk, q_block, 0)


def _tri(blk):
    r = lax.broadcasted_iota(jnp.int32, (blk, blk), 0)
    c = lax.broadcasted_iota(jnp.int32, (blk, blk), 1)
    return (r >= c).astype(BF16)


def _sb_prompt(q, kv, bias, *, batch, seq, heads, blk, hpb=4):
    m = q.shape[0]
    hpb = hpb if heads % hpb == 0 else 1
    w = hpb * HEAD_DIM
    bias_b = jnp.broadcast_to(LOG2E * bias.astype(F32)[:, None, None], (heads, 1, blk))
    return pl.pallas_call(
        functools.partial(_sb_prompt_kernel, seq=seq, blk=blk, hpb=hpb),
        grid=(batch, heads // hpb),
        in_specs=[pl.BlockSpec((seq, w), lambda n, h: (n, h)),
                  pl.BlockSpec((seq, w), lambda n, h: (n, h)),
                  pl.BlockSpec((seq, w), lambda n, h: (n, heads // hpb + h)),
                  pl.BlockSpec((hpb, 1, blk), lambda n, h: (h, 0, 0)),
                  pl.BlockSpec((blk, blk), lambda n, h: (0, 0))],
        out_specs=pl.BlockSpec((seq, w), lambda n, h: (n, h)),
        out_shape=jax.ShapeDtypeStruct((m, heads * HEAD_DIM), BF16),
        scratch_shapes=[pltpu.VMEM((seq, w), BF16), pltpu.VMEM((seq, w), BF16)],
        compiler_params=_params(("arbitrary", "arbitrary")),
        name="sb_prompt",
    )(q, kv, kv, bias_b, _tri(blk))


QROWS = SUBLANES
PAGES_PER_STEP = 4


def _sb_sample_kernel(pt_ref, q_ref, new_ref, *refs, heads, dec_seq, page):
    del pt_ref
    page_refs = refs[:PAGES_PER_STEP]
    b_ref, tri_ref, o_ref, q3_ref, acc_ref, carry_ref = refs[PAGES_PER_STEP:]
    p = pl.program_id(1)
    rows = heads * QROWS

    def block(kv_ref, mask):
        x = kv_ref[...].astype(BF16).reshape(page, 2 * heads, HEAD_DIM)
        k = pltpu.einshape("khd->hkd", x[:, 0:heads, :])
        v = pltpu.einshape("khd->hkd", x[:, heads:2 * heads, :])
        s = jnp.einsum("htd,hkd->htk", q3_ref[...].astype(BF16), k, preferred_element_type=F32)
        x2 = s.reshape(rows, page) + b_ref[...]
        neg_abs = pltpu.bitcast(pltpu.bitcast(x2, jnp.uint32) | jnp.uint32(0x80000000), F32)
        sp = LN2 * jnp.maximum(x2, 0.0) + jnp.log(1.0 + jnp.exp2(neg_abs))
        if mask is not None:
            sp = jnp.where(mask, sp, 0.0)
        incl = jnp.dot(sp.astype(BF16), tri_ref[...], preferred_element_type=F32)
        a = jnp.exp2(x2 - LOG2E * incl)
        if mask is not None:
            a = jnp.where(mask, a, 0.0)
        pv = jnp.einsum("htk,hkd->htd", a.astype(BF16).reshape(heads, QROWS, page), v,
                        preferred_element_type=F32).reshape(rows, HEAD_DIM)
        return pv, jnp.sum(sp, axis=1, keepdims=True)

    def accumulate(results):
        carry = carry_ref[:, 0:1]
        acc = acc_ref[...]
        for pv, mass in results:
            acc = acc + jnp.exp(-carry) * pv
            carry = carry + mass
        acc_ref[...] = acc
        carry_ref[...] = jnp.broadcast_to(carry, carry_ref.shape)

    @pl.when(p == 0)
    def _():
        q3_ref[...] = jnp.zeros_like(q3_ref)
        for h in range(heads):
            q3_ref[h, 0:dec_seq, :] = q_ref[:, h * HEAD_DIM:(h + 1) * HEAD_DIM]
        acc_ref[...] = jnp.zeros_like(acc_ref)
        carry_ref[...] = jnp.zeros_like(carry_ref)
        t_of_row = lax.broadcasted_iota(jnp.int32, (rows, page), 0) % QROWS
        key = lax.broadcasted_iota(jnp.int32, (rows, page), 1)
        accumulate([block(new_ref, key < t_of_row)])

    @pl.when(p > 0)
    def _():
        accumulate([block(r, None) for r in page_refs])

    @pl.when(p == pl.num_programs(1) - 1)
    def _():
        for h in range(heads):
            o_ref[:, h * HEAD_DIM:(h + 1) * HEAD_DIM] = acc_ref[h * QROWS:h * QROWS + dec_seq, :]


def _sb_sample(q, kv_new_pad, pages, page_table, bias, *, heads, page):
    db, dec_seq, width = q.shape
    n_pages = page_table.shape[1]
    pps = PAGES_PER_STEP
    assert dec_seq <= QROWS and n_pages % pps == 0
    rows = heads * QROWS
    prow = page * 2 * heads
    bias_b = jnp.broadcast_to(LOG2E * jnp.repeat(bias.astype(F32), QROWS)[:, None], (rows, page))

    def page_map(which):
        return lambda n, p, pt: (pt[n, n_pages - pps * (jnp.maximum(p, 1) - 1) - 1 - which], 0, 0)

    grid_spec = pltpu.PrefetchScalarGridSpec(
        num_scalar_prefetch=1,
        grid=(db, n_pages // pps + 1),
        in_specs=[pl.BlockSpec((None, dec_seq, width), lambda n, p, pt: (n, 0, 0)),
                  pl.BlockSpec((None, prow, HEAD_DIM), lambda n, p, pt: (n, 0, 0)),
                  *[pl.BlockSpec((None, prow, HEAD_DIM), page_map(w)) for w in range(pps)],
                  pl.BlockSpec((rows, page), lambda n, p, pt: (0, 0)),
                  pl.BlockSpec((page, page), lambda n, p, pt: (0, 0))],
        out_specs=pl.BlockSpec((None, dec_seq, width), lambda n, p, pt: (n, 0, 0)),
        scratch_shapes=[pltpu.VMEM((heads, QROWS, HEAD_DIM), F32), pltpu.VMEM((rows, HEAD_DIM), F32),
                        pltpu.VMEM((rows, HEAD_DIM), F32)],
    )
    return pl.pallas_call(
        functools.partial(_sb_sample_kernel, heads=heads, dec_seq=dec_seq, page=page),
        grid_spec=grid_spec,
        out_shape=jax.ShapeDtypeStruct((db, dec_seq, width), F32),
        compiler_params=_params(("arbitrary", "arbitrary")),
        name="sb_sample",
    )(page_table, q, kv_new_pad, *([pages] * pps), bias_b, _tri(page))


def _rope_table(pos):
    half = ROT_DIM // 2
    inv = ROPE_THETA ** (-jnp.arange(half, dtype=F32) / half)
    ang = pos.astype(F32)[:, None] * inv[None, :]
    cos, sin = jnp.cos(ang), jnp.sin(ang)
    t = pos.shape[0]
    ones = jnp.ones((t, HEAD_DIM - ROT_DIM), F32)
    zeros = jnp.zeros((t, HEAD_DIM - ROT_DIM), F32)
    return jnp.concatenate([cos, cos, ones, -sin, sin, zeros], axis=1)


def _pick(v, cands):
    for c in cands:
        if v % c == 0:
            return c
    return v


def _trunk(xp, xs, mods_p, mods_s, modkv_p, modkv_s, tabs_p, tabs_s, prev, cfg, weights):
    (norm_attn_g, norm_ffn_g, w_qkv_a, w_o_a, w_q_b, w_o_b, sb_bias, norm_kv_g, w_kv,
     w_up, w_conv, b_conv3, w_down, norm_final_g) = weights
    d = xp.shape[1]
    ms = xs.shape[0]
    bm, bmn, seq = cfg["bm"], cfg["bm_norm"], cfg["seq"]
    depth = w_up.shape[0]
    n_a = w_qkv_a.shape[0]
    heads_a = w_o_a.shape[1] // HEAD_DIM
    d_ff = w_down.shape[1]

    def norm_both(g, sc_p, sh_p, sc_s, sh_s):
        return (_norm_mod(xp, g, sc_p, sh_p, bm=bmn, rows_per_mod=seq),
                _norm_mod(xs, g, sc_s, sh_s, bm=ms, rows_per_mod=ms))

    def tile(n, cands=(512, 256, 128)):
        return _pick(n, cands)

    def side_rows(bn):
        return (pl.BlockSpec((ms, bn), lambda j, i: (0, j)), pl.BlockSpec((None, ms, bn), lambda j, i: (0, 0, j)))

    qkv_p, qkv_s, tails_p, tails_s = [], [], [], []
    kv_p = kv_s = None
    for l in range(depth):
        sh1, sc1, g1, sh2, sc2, g2 = mods_p[l]
        sh1s, sc1s, g1s, sh2s, sc2s, g2s = mods_s[l]
        hp, hs = norm_both(norm_attn_g[l][None], sc1, sh1, sc1s, sh1s)
        if l < n_a:
            bn = tile(heads_a * HEAD_DIM)
            qp, qs = _matmul(hp, w_qkv_a, l, bm=bm, bn=bn, out_dtype=F32, mode="rope",
                             extras=(tabs_p,), extra_specs=(pl.BlockSpec((bm, 2 * HEAD_DIM), cfg["tab_map"]),),
                             side=(hs, (tabs_s,), (pl.BlockSpec((ms, 2 * HEAD_DIM), lambda j, i: (0, 0)),), F32),
                             qkv_width=heads_a * HEAD_DIM, name="qkv")
            qkv_p.append(qp)
            qkv_s.append(qs)
            op, os_ = cfg["attn_a_p"](qp), cfg["attn_a_s"](qs, l)
            w_o, lo = w_o_a, l
        else:
            j = l - n_a
            qp, qs = _matmul(hp, w_q_b, j, bm=bm, bn=tile(w_q_b.shape[2]), out_dtype=BF16,
                             out_scale=SB_QSCALE, side=(hs, (), (), F32), name="q_b")
            op, os_ = cfg["attn_b_p"](qp, kv_p, sb_bias[j]), cfg["attn_b_s"](qs, kv_s, sb_bias[j])
            w_o, lo = w_o_b, j
        bn = tile(d)
        xp, xs = _matmul(op, w_o, lo, bm=bm, bn=bn, out_dtype=F32, mode="resid", extras=(xp, g1),
                         extra_specs=(pl.BlockSpec((bm, bn), lambda j, i: (i, j)),
                                      pl.BlockSpec((None, 1, bn), lambda j, i: ((i * bm) // seq, 0, j))),
                         side=(os_, (xs, g1s), side_rows(bn), F32), name="attn_out")
        hp, hs = norm_both(norm_ffn_g[l][None], sc2, sh2, sc2s, sh2s)
        act_p, tg, tu, act_s, tgs, tus = _ffn_up(hp, w_up, w_conv, b_conv3, l, bm=bm, bn=_pick(d_ff, (256, 128)),
                                                 rows_per_seq=seq, side=(hs, prev[l], cfg["tap_s"]))
        tails_p.append((tg, tu))
        tails_s.append((tgs, tus))
        xp, xs = _matmul_ktiled_resid(act_p, w_down, l, xp, g2, bm=cfg["bm_down"],
                                      bn=_pick(d, (cfg["bn_down"], 512, 256, 128)), rows_per_mod=seq,
                                      side=(act_s, xs, g2s))
        if l == n_a - 1:
            hp, hs = norm_both(norm_kv_g[None], modkv_p[1], modkv_p[0], modkv_s[1], modkv_s[0])
            kv_p, kv_s = _matmul(hp, w_kv[None], 0, bm=bm, bn=tile(w_kv.shape[1]), out_dtype=F32,
                                 side=(hs, (), (), F32), name="kv")
    y_p = _final_norm(xp, norm_final_g[None], bm=bmn)
    y_s = _final_norm(xs, norm_final_g[None], bm=ms)
    return (y_p, qkv_p, kv_p, tails_p), (y_s, qkv_s, kv_s, tails_s)


def kernel(x_prompt, x_sample, cache_win0, cache_win1, cache_win2, cache_kv_pages, state_conv, page_table,
           c_prompt, c_sample, norm_attn_g, norm_ffn_g, w_ada, b_ada, w_qkv_a, w_o_a, w_q_b, w_o_b, sb_bias,
           norm_kv_g, w_ada_kv, b_ada_kv, w_kv, w_up, w_conv, b_conv, w_down, norm_final_g):
    batch, seq, d = x_prompt.shape
    db, dec_seq, _ = x_sample.shape
    depth = w_up.shape[0]
    n_a = w_qkv_a.shape[0]
    heads_a = w_o_a.shape[1] // HEAD_DIM
    heads_b = w_q_b.shape[2] // HEAD_DIM
    d_ff = w_down.shape[1]
    page = cache_kv_pages.shape[1]
    past_len = page_table.shape[1] * page
    caches = (cache_win0, cache_win1, cache_win2)
    assert seq % (BAND * DILATIONS[-1]) == 0 and batch + db <= ADA_ROWS and db == SUBLANES
    assert all(caches[g].shape[2] == WINDOWS[g] for g in range(N_GROUPS_A)) and dec_seq <= DILATIONS[1]

    c_all = jnp.concatenate([c_prompt, c_sample, jnp.zeros((ADA_ROWS - batch - db, d), F32)], axis=0)

    def ada(w, b, layer):
        n = w.shape[2]
        bn = _pick(n, (1024, 512, 256, 128))
        return _matmul(c_all, w, layer, bm=ADA_ROWS, bn=bn, out_dtype=F32, mode="bias", silu_in=True,
                       extras=(b.reshape(w.shape[0], 1, n),),
                       extra_specs=(pl.BlockSpec((None, 1, bn), lambda j, i: (layer, 0, j)),), name="ada")

    mod_all = [ada(w_ada, b_ada, l) for l in range(depth)]
    mod_kv_all = ada(w_ada_kv[None], b_ada_kv[None], 0)

    def prompt_mods(a, chunks):
        return tuple(c.reshape(batch, 1, d) for c in jnp.split(a[:batch], chunks, axis=-1))

    def sample_mods(a, chunks):
        return tuple(jnp.tile(c, (dec_seq, 1))[None] for c in jnp.split(a[batch:batch + db], chunks, axis=-1))

    b_conv3 = b_conv.reshape(depth, 1, 2 * d_ff)
    weights = (norm_attn_g, norm_ffn_g, w_qkv_a, w_o_a, w_q_b, w_o_b, sb_bias, norm_kv_g, w_kv,
               w_up, w_conv, b_conv3, w_down, norm_final_g)

    m_s = dec_seq * db
    x_s = x_sample.transpose(1, 0, 2).reshape(m_s, d)
    pages = cache_kv_pages.reshape(cache_kv_pages.shape[0], page * 2 * heads_b, HEAD_DIM)
    prev = state_conv.transpose(0, 2, 1, 3).reshape(depth, (CONV_W - 1) * db, 2 * d_ff)

    def n_major(a):
        return a.reshape(dec_seq, db, a.shape[-1]).transpose(1, 0, 2)

    def t_major(a):
        return a.transpose(1, 0, 2).reshape(m_s, a.shape[-1])

    def attn_b_sample(q, kv, bias):
        kv_new = n_major(kv).reshape(db, dec_seq * 2 * heads_b, HEAD_DIM)
        kv_pad = jnp.concatenate([kv_new, jnp.zeros((db, (page - dec_seq) * 2 * heads_b, HEAD_DIM), F32)], axis=1)
        return t_major(_sb_sample(n_major(q), kv_pad, pages, page_table, bias, heads=heads_b, page=page))

    bm_p = _pick(seq, (1024, 512, 256, 128))
    sb_blk = _pick(seq, (256, 128))
    cfg = dict(
        seq=seq, bm=bm_p, bm_norm=_pick(seq, (256, 128)), bm_down=_pick(seq, (2048, 1024, 512, 256, 128)),
        bn_down=1024, tap_s=db,
        tab_map=lambda j, i: (i % (seq // bm_p), 0),
        attn_a_p=lambda qkv: _attn_a_prompt(qkv, batch=batch, seq=seq, heads=heads_a),
        attn_b_p=lambda q, kv, bias: _sb_prompt(q, kv, bias, batch=batch, seq=seq, heads=heads_b, blk=sb_blk),
        attn_a_s=lambda qkv, l: t_major(_attn_a_sample(n_major(qkv), caches, l, heads=heads_a)),
        attn_b_s=attn_b_sample,
    )
    pos_s = past_len + jnp.repeat(jnp.arange(dec_seq), db)
    (y_p, qkv_p, kv_p, tails_p), (y_s, qkv_s, kv_s, tails_s) = _trunk(
        x_prompt.reshape(batch * seq, d), x_s,
        [prompt_mods(mod_all[l], 6) for l in range(depth)], [sample_mods(mod_all[l], 6) for l in range(depth)],
        prompt_mods(mod_kv_all, 2), sample_mods(mod_kv_all, 2),
        _rope_table(jnp.arange(seq)), _rope_table(pos_s), prev, cfg, weights)

    win_p = []
    for g in range(N_GROUPS_A):
        keep = min(WINDOWS[g], seq)
        per_layer = [q.reshape(batch, seq, N_GROUPS_A, 3, heads_a, HEAD_DIM)[:, seq - keep:, g, 1:3]
                     for q in qkv_p]
        win_p.append(jnp.stack(per_layer, axis=0))
    kv_prompt = kv_p.reshape(batch, seq, 2, heads_b, HEAD_DIM)
    nblk = seq // bm_p
    conv_p = jnp.stack([
        jnp.concatenate([tg.reshape(batch, nblk, SUBLANES, d_ff)[:, -1, SUBLANES - (CONV_W - 1):],
                         tu.reshape(batch, nblk, SUBLANES, d_ff)[:, -1, SUBLANES - (CONV_W - 1):]], axis=-1)
        for tg, tu in tails_p], axis=0)

    qkv_new = jnp.stack([n_major(q) for q in qkv_s], axis=0).reshape(
        n_a, db, dec_seq, N_GROUPS_A, 3, heads_a, HEAD_DIM)
    win_s = []
    for g in range(N_GROUPS_A):
        c = caches[g]
        pads = [(0, 0, 0)] * c.ndim
        pads[2] = (-dec_seq, dec_seq, 0)
        win_s.append(lax.dynamic_update_slice(lax.pad(c, jnp.zeros((), c.dtype), pads), qkv_new[:, :, :, g, 1:3],
                                              (0, 0, c.shape[2] - dec_seq, 0, 0, 0)))
    kv_sample = n_major(kv_s).reshape(db, dec_seq, 2, heads_b, HEAD_DIM)
    conv_s = jnp.stack([
        jnp.concatenate([tg, tu], axis=-1).reshape(CONV_W - 1, db, 2 * d_ff).transpose(1, 0, 2)
        for tg, tu in tails_s], axis=0)

    y_prompt = y_p.reshape(batch, seq, d)
    y_sample = y_s.reshape(dec_seq, db, d).transpose(1, 0, 2)
    return (y_prompt, y_sample, win_p[0], win_s[0], win_p[1], win_s[1], win_p[2], win_s[2],
            kv_prompt, kv_sample, conv_p, conv_s)
```
